```python
import jax, jax.numpy as jnp
from jax import lax
import numpy as np

D_MODEL = 1024
BATCH = 4
SEQ = 8192
DEPTH = 1

MLA_HEADS = 8
QK_NOPE_DIM = 128
QK_ROPE_DIM = 64
QK_HEAD_DIM = QK_NOPE_DIM + QK_ROPE_DIM
V_HEAD_DIM = 128
Q_LORA_RANK = 256
KV_LORA_RANK = 256
ROPE_THETA = 10000.0
Q_BLOCK = 128

LRU_WIDTH = D_MODEL
LRU_BLOCKS = 16
LRU_BLOCK_DIM = LRU_WIDTH // LRU_BLOCKS
CONV_WIDTH = 4
LRU_C = 8.0

FFN_HIDDEN = -(-8 * D_MODEL // (3 * 256)) * 256

EPS = 1e-6

COL_Q = Q_LORA_RANK
COL_KV = KV_LORA_RANK + QK_ROPE_DIM
COL_LRU_X = LRU_WIDTH
COL_LRU_G = LRU_WIDTH
COL_GATE_A = D_MODEL
COL_GATE_B = D_MODEL
IN_WIDTH = COL_Q + COL_KV + COL_LRU_X + COL_LRU_G + COL_GATE_A + COL_GATE_B

kernel_name = "hybrid_mla_rglru_gated_block"


def rms_norm(x, g):
    xf = x.astype(jnp.float32)
    xf = xf * lax.rsqrt(jnp.mean(xf * xf, axis=-1, keepdims=True) + EPS)
    return xf.astype(x.dtype) * g


def rope_tables(positions):
    half = QK_ROPE_DIM // 2
    inv_freq = ROPE_THETA ** (-jnp.arange(half, dtype=jnp.float32) / half)
    ang = positions.astype(jnp.float32)[..., None] * inv_freq
    return jnp.cos(ang)[:, :, None, :], jnp.sin(ang)[:, :, None, :]


def apply_rope(x, cos, sin):
    half = QK_ROPE_DIM // 2
    xf = x.astype(jnp.float32)
    x1, x2 = xf[..., :half], xf[..., half:]
    return jnp.concatenate([x1 * cos - x2 * sin, x2 * cos + x1 * sin], axis=-1).astype(x.dtype)


def head_qk_norm(t, g):
    return jnp.concatenate([rms_norm(t[..., :QK_NOPE_DIM], g[:QK_NOPE_DIM]),
                            rms_norm(t[..., QK_NOPE_DIM:], g[QK_NOPE_DIM:])], axis=-1)


def causal_attention_blocks(q, k, v):
    B, H, S, Dq = q.shape
    nblk = S // Q_BLOCK
    scale = QK_HEAD_DIM ** -0.5
    q_blocks = q.reshape(B, H, nblk, Q_BLOCK, Dq).transpose(2, 0, 1, 3, 4)
    key_idx = jnp.arange(S)

    def one_block(args):
        qb, bi = args
        s = jnp.einsum('bhqd,bhkd->bhqk', qb, k).astype(jnp.float32) * scale
        q_idx = bi * Q_BLOCK + jnp.arange(Q_BLOCK)
        mask = key_idx[None, :] <= q_idx[:, None]
        s = jnp.where(mask, s, -jnp.inf)
        p = jax.nn.softmax(s, axis=-1).astype(v.dtype)
        return jnp.einsum('bhqk,bhkd->bhqd', p, v)

    out = lax.map(one_block, (q_blocks, jnp.arange(nblk)))
    return out.transpose(1, 0, 3, 2, 4).reshape(B, S, H * V_HEAD_DIM)


def mla_branch(c_q_raw, ckv_raw, cos, sin, q_lat_g, w_q_up, kv_lat_g, w_kv_up, q_head_g, k_head_g):
    B, S, _ = c_q_raw.shape
    c_q = rms_norm(c_q_raw, q_lat_g)
    q = (c_q @ w_q_up).reshape(B, S, MLA_HEADS, QK_HEAD_DIM)
    c_kv = rms_norm(ckv_raw[..., :KV_LORA_RANK], kv_lat_g)
    k_rope = ckv_raw[..., KV_LORA_RANK:][:, :, None, :]
    kv = (c_kv @ w_kv_up).reshape(B, S, MLA_HEADS, QK_NOPE_DIM + V_HEAD_DIM)
    k_nope, v = kv[..., :QK_NOPE_DIM], kv[..., QK_NOPE_DIM:]
    q = head_qk_norm(q, q_head_g)
    k_nope = rms_norm(k_nope, k_head_g[:QK_NOPE_DIM])
    k_rope = rms_norm(k_rope, k_head_g[QK_NOPE_DIM:])
    q = jnp.concatenate([q[..., :QK_NOPE_DIM], apply_rope(q[..., QK_NOPE_DIM:], cos, sin)], axis=-1)
    k_rope = jnp.broadcast_to(apply_rope(k_rope, cos, sin), (B, S, MLA_HEADS, QK_ROPE_DIM))
    k = jnp.concatenate([k_nope, k_rope], axis=-1)
    return causal_attention_blocks(q.transpose(0, 2, 1, 3), k.transpose(0, 2, 1, 3), v.transpose(0, 2, 1, 3))


def rglru_branch(x_b, g_b, conv_w, conv_b, lru_wa, lru_ba, lru_wx, lru_bx, lru_lambda):
    B, S, _ = x_b.shape
    xp = jnp.pad(x_b, ((0, 0), (CONV_WIDTH - 1, 0), (0, 0)))
    xc = conv_b + xp[:, 0:S] * conv_w[0]
    for tap in range(1, CONV_WIDTH):
        xc = xc + xp[:, tap:tap + S] * conv_w[tap]
    xr = xc.reshape(B, S, LRU_BLOCKS, LRU_BLOCK_DIM)
    r = jax.nn.sigmoid(jnp.einsum('bsnd,nde->bsne', xr, lru_wa).reshape(B, S, LRU_WIDTH) + lru_ba)
    i = jax.nn.sigmoid(jnp.einsum('bsnd,nde->bsne', xr, lru_wx).reshape(B, S, LRU_WIDTH) + lru_bx)
    log_a = -LRU_C * r.astype(jnp.float32) * jax.nn.softplus(-lru_lambda.astype(jnp.float32))
    a = jnp.exp(log_a)
    mult = jnp.sqrt(-jnp.expm1(2.0 * log_a))
    b = mult * (i * xc).astype(jnp.float32)

    def combine(left, right):
        a1, b1 = left
        a2, b2 = right
        return a1 * a2, a2 * b1 + b2

    _, h = lax.associative_scan(combine, (a, b), axis=1)
    return h.astype(x_b.dtype) * jax.nn.gelu(g_b)


def setup_inputs(seed: int = 0) -> dict:
    key = jax.random.key(seed)
    ks = jax.random.split(key, 24)
    f32 = jnp.float32
    L = DEPTH

    def nrm(k, shape, fan_in):
        return jax.random.normal(k, shape, f32) * (fan_in ** -0.5)

    def gain(k, shape):
        return 1.0 + 0.02 * jax.random.normal(k, shape, f32)

    x = jax.random.normal(ks[0], (BATCH, SEQ, D_MODEL), f32)
    offsets = jax.random.randint(ks[1], (BATCH, 1), 0, 1024, dtype=jnp.int32)
    positions = offsets + jnp.arange(SEQ, dtype=jnp.int32)[None, :]
    a0 = jax.random.uniform(ks[17], (L, LRU_WIDTH), f32, 0.9, 0.999)
    return {
        "x": x,
        "positions": positions,
        "norm_mix_g": gain(ks[2], (L, D_MODEL)),
        "w_in": nrm(ks[3], (L, D_MODEL, IN_WIDTH), D_MODEL),
        "q_lat_g": gain(ks[4], (L, Q_LORA_RANK)),
        "w_q_up": nrm(ks[5], (L, Q_LORA_RANK, MLA_HEADS * QK_HEAD_DIM), Q_LORA_RANK),
        "kv_lat_g": gain(ks[6], (L, KV_LORA_RANK)),
        "w_kv_up": nrm(ks[7], (L, KV_LORA_RANK, MLA_HEADS * (QK_NOPE_DIM + V_HEAD_DIM)), KV_LORA_RANK),
        "q_head_g": gain(ks[8], (L, QK_HEAD_DIM)),
        "k_head_g": gain(ks[9], (L, QK_HEAD_DIM)),
        "conv_w": nrm(ks[10], (L, CONV_WIDTH, LRU_WIDTH), CONV_WIDTH),
        "conv_b": 0.02 * jax.random.normal(ks[11], (L, LRU_WIDTH), f32),
        "lru_wa": nrm(ks[12], (L, LRU_BLOCKS, LRU_BLOCK_DIM, LRU_BLOCK_DIM), LRU_BLOCK_DIM),
        "lru_ba": 0.02 * jax.random.normal(ks[13], (L, LRU_WIDTH), f32),
        "lru_wx": nrm(ks[14], (L, LRU_BLOCKS, LRU_BLOCK_DIM, LRU_BLOCK_DIM), LRU_BLOCK_DIM),
        "lru_bx": 0.02 * jax.random.normal(ks[15], (L, LRU_WIDTH), f32),
        "lru_lambda": jnp.log(a0) - jnp.log1p(-a0),
        "w_proj_attn": nrm(ks[16], (L, MLA_HEADS * V_HEAD_DIM, D_MODEL), MLA_HEADS * V_HEAD_DIM),
        "w_proj_lru": nrm(ks[18], (L, LRU_WIDTH, D_MODEL), LRU_WIDTH),
        "w_out": nrm(ks[19], (L, D_MODEL, D_MODEL), D_MODEL),
        "norm_ffn_g": gain(ks[20], (L, D_MODEL)),
        "w_ffn_gate": nrm(ks[21], (L, D_MODEL, FFN_HIDDEN), D_MODEL),
        "w_ffn_up": nrm(ks[22], (L, D_MODEL, FFN_HIDDEN), D_MODEL),
        "w_ffn_down": nrm(ks[23], (L, FFN_HIDDEN, D_MODEL), FFN_HIDDEN),
    }


def reference(x, positions, norm_mix_g, w_in, q_lat_g, w_q_up, kv_lat_g, w_kv_up, q_head_g, k_head_g,
              conv_w, conv_b, lru_wa, lru_ba, lru_wx, lru_bx, lru_lambda, w_proj_attn, w_proj_lru,
              w_out, norm_ffn_g, w_ffn_gate, w_ffn_up, w_ffn_down):
    split_points = [COL_Q, COL_Q + COL_KV, COL_Q + COL_KV + COL_LRU_X,
                    COL_Q + COL_KV + COL_LRU_X + COL_LRU_G,
                    COL_Q + COL_KV + COL_LRU_X + COL_LRU_G + COL_GATE_A]
    cos, sin = rope_tables(positions)
    for l in range(DEPTH):
        h = rms_norm(x, norm_mix_g[l])
        proj = h @ w_in[l]
        c_q_raw, ckv_raw, x_lru, g_lru, gate_a, gate_b = jnp.split(proj, split_points, axis=-1)
        y_a = mla_branch(c_q_raw, ckv_raw, cos, sin, q_lat_g[l], w_q_up[l], kv_lat_g[l], w_kv_up[l],
                         q_head_g[l], k_head_g[l])
        y_b = rglru_branch(x_lru, g_lru, conv_w[l], conv_b[l], lru_wa[l], lru_ba[l], lru_wx[l],
                           lru_bx[l], lru_lambda[l])
        merged = (jax.nn.sigmoid(gate_a) * (y_a @ w_proj_attn[l])
                  + jax.nn.sigmoid(gate_b) * (y_b @ w_proj_lru[l]))
        x = x + merged @ w_out[l]
        h2 = rms_norm(x, norm_ffn_g[l])
        x = x + (jax.nn.silu(h2 @ w_ffn_gate[l]) * (h2 @ w_ffn_up[l])) @ w_ffn_down[l]
    return x
```

```python
import functools
import math

import jax
import jax.numpy as jnp
from jax import lax
from jax.experimental import pallas as pl
from jax.experimental.pallas import tpu as pltpu

D_MODEL = 1024
HEADS = 8
NOPE = 128
ROPE = 64
HALF_ROPE = ROPE // 2
QK_DIM = NOPE + ROPE
V_DIM = 128
Q_RANK = 256
KV_RANK = 256
ROPE_THETA = 10000.0
LRU_WIDTH = 1024
LRU_BLOCK = 64
CONV_WIDTH = 4
LRU_C = 8.0
EPS = 1e-6

LANES = 128
SUBLANES = 8
MXU_DIM = 256
QK_PAD = 2 * LANES

VMEM_LIMIT = 56 * 1024 * 1024

TM_PROJ = 512
TM_PREP = 512
TQ = 512
TK = 512
TS = 256

F32 = jnp.float32
BF16 = jnp.bfloat16


def _params(*semantics):
    return pltpu.CompilerParams(dimension_semantics=semantics, vmem_limit_bytes=VMEM_LIMIT)


def _const_spec(shape):
    zeros = (0,) * len(shape)
    return pl.BlockSpec(shape, lambda *_: zeros)


def _sigmoid(v):
    return 1.0 / (1.0 + jnp.exp(-v))


def _rms_rows(v, gain_row):
    ms = jnp.mean(v * v, axis=-1, keepdims=True)
    return v * lax.rsqrt(ms + EPS) * gain_row


def _rms_cols(v, gain_col):
    ms = jnp.mean(v * v, axis=0, keepdims=True)
    return v * lax.rsqrt(ms + EPS) * gain_col


def _dot(a, b):
    return jnp.dot(a, b, preferred_element_type=F32)


def _dot_nt(a, b):
    return lax.dot_general(a, b, (((1,), (1,)), ((), ())), preferred_element_type=F32)


def _in_proj_kernel(x_ref, g_ref, wq_ref, wkv_ref, wkr_ref, wx_ref, wg_ref, wa_ref, wb_ref,
                    cq_ref, ckv_ref, kr_ref, xl_ref, gl_ref, ga_ref, gb_ref):
    h = _rms_rows(x_ref[...], g_ref[...]).astype(BF16)
    cq_ref[...] = _dot(h, wq_ref[...])
    ckv_ref[...] = _dot(h, wkv_ref[...])
    kr_ref[...] = _dot(h, wkr_ref[...])
    xl_ref[...] = _dot(h, wx_ref[...])
    gl_ref[...] = _dot(h, wg_ref[...]).astype(BF16)
    ga_ref[...] = _dot(h, wa_ref[...]).astype(BF16)
    gb_ref[...] = _dot(h, wb_ref[...]).astype(BF16)


def _in_proj(x2d, norm_g, w_parts):
    m = x2d.shape[0]
    widths = [w.shape[1] for w in w_parts]
    dtypes = [F32, F32, F32, F32, BF16, BF16, BF16]
    row = lambda i: (i, 0)
    return pl.pallas_call(
        _in_proj_kernel,
        grid=(m // TM_PROJ,),
        in_specs=[pl.BlockSpec((TM_PROJ, D_MODEL), row), _const_spec((1, D_MODEL))]
        + [_const_spec(w.shape) for w in w_parts],
        out_specs=[pl.BlockSpec((TM_PROJ, n), row) for n in widths],
        out_shape=[jax.ShapeDtypeStruct((m, n), dt) for n, dt in zip(widths, dtypes)],
        compiler_params=_params("parallel"),
        name="in_proj",
    )(x2d, norm_g, *w_parts)


def _mla_prep_kernel(cq_ref, ckv_ref, kr_ref, pos_ref, freq_ref, qlg_ref, kvlg_ref,
                     wqt_ref, wkn_ref, wvt_ref, gqn_ref, gqr_ref, gkn_ref, gkr_ref,
                     qt_ref, kn_ref, krope_ref, vt_ref):
    scale = QK_DIM ** -0.5
    ang = freq_ref[...] * pos_ref[0].astype(F32)
    cos = jnp.cos(ang)
    sin = jnp.sin(ang)

    def rope_cols(v):
        x1, x2 = v[:HALF_ROPE], v[HALF_ROPE:]
        return x1 * cos - x2 * sin, x2 * cos + x1 * sin

    cqn = _rms_rows(cq_ref[...], qlg_ref[...]).astype(BF16)
    q_t = _dot_nt(wqt_ref[...], cqn)
    pad = jnp.zeros((QK_PAD - QK_DIM, q_t.shape[1]), BF16)
    for h in range(HEADS):
        base = h * QK_DIM
        qn = _rms_cols(q_t[base:base + NOPE], gqn_ref[...]) * scale
        qr = _rms_cols(q_t[base + NOPE:base + QK_DIM], gqr_ref[...]) * scale
        o1, o2 = rope_cols(qr)
        qt_ref[0, h, 0:NOPE, :] = qn.astype(BF16)
        qt_ref[0, h, NOPE:NOPE + HALF_ROPE, :] = o1.astype(BF16)
        qt_ref[0, h, NOPE + HALF_ROPE:QK_DIM, :] = o2.astype(BF16)
        qt_ref[0, h, QK_DIM:QK_PAD, :] = pad

    ckvn = _rms_rows(ckv_ref[...], kvlg_ref[...]).astype(BF16)
    kn_all = _dot(ckvn, wkn_ref[...])
    v_t = _dot_nt(wvt_ref[...], ckvn)
    n_chunks = vt_ref.shape[2]
    for h in range(HEADS):
        kn = _rms_rows(kn_all[:, h * NOPE:(h + 1) * NOPE], gkn_ref[...])
        kn_ref[0, h] = kn.astype(BF16)
        for c in range(n_chunks):
            vt_ref[0, h, c] = v_t[h * V_DIM:(h + 1) * V_DIM, c * TK:(c + 1) * TK].astype(BF16)

    kr_t = kr_ref[...].T
    krn = _rms_cols(kr_t[:ROPE], gkr_ref[...])
    o1, o2 = rope_cols(krn)
    kr_out = jnp.concatenate([o1, o2, jnp.zeros((LANES - ROPE, o1.shape[1]), F32)], axis=0)
    krope_ref[0] = kr_out.T.astype(BF16)


def _mla_prep(cq, ckv, kr, positions3, inv_freq, q_lat_g, kv_lat_g, wq_t, wkn, wv_t,
              gqn, gqr, gkn, gkr, batch, seq):
    tm = TM_PREP
    nblk = seq // tm
    tok = lambda b, i: (b * nblk + i, 0)
    in_specs = [
        pl.BlockSpec((tm, Q_RANK), tok),
        pl.BlockSpec((tm, KV_RANK), tok),
        pl.BlockSpec((tm, LANES), tok),
        pl.BlockSpec((1, 1, tm), lambda b, i: (b, 0, i)),
        _const_spec(inv_freq.shape), _const_spec(q_lat_g.shape), _const_spec(kv_lat_g.shape),
        _const_spec(wq_t.shape), _const_spec(wkn.shape), _const_spec(wv_t.shape),
        _const_spec(gqn.shape), _const_spec(gqr.shape), _const_spec(gkn.shape), _const_spec(gkr.shape),
    ]
    out_specs = [
        pl.BlockSpec((1, HEADS, QK_PAD, tm), lambda b, i: (b, 0, 0, i)),
        pl.BlockSpec((1, HEADS, tm, NOPE), lambda b, i: (b, 0, i, 0)),
        pl.BlockSpec((1, tm, LANES), lambda b, i: (b, i, 0)),
        pl.BlockSpec((1, HEADS, tm // TK, V_DIM, TK), lambda b, i: (b, 0, i, 0, 0)),
    ]
    out_shape = [
        jax.ShapeDtypeStruct((batch, HEADS, QK_PAD, seq), BF16),
        jax.ShapeDtypeStruct((batch, HEADS, seq, NOPE), BF16),
        jax.ShapeDtypeStruct((batch, seq, LANES), BF16),
        jax.ShapeDtypeStruct((batch, HEADS, seq // TK, V_DIM, TK), BF16),
    ]
    return pl.pallas_call(
        _mla_prep_kernel,
        grid=(batch, nblk),
        in_specs=in_specs,
        out_specs=out_specs,
        out_shape=out_shape,
        compiler_params=_params("parallel", "parallel"),
        name="mla_prep",
    )(cq, ckv, kr, positions3, inv_freq, q_lat_g, kv_lat_g, wq_t, wkn, wv_t, gqn, gqr, gkn, gkr)


def _attention_kernel(qt_ref, kn_ref, kr_ref, vt_ref, o_ref, m_ref, l_ref, acc_ref):
    i = pl.program_id(2)
    q_t = qt_ref[0, 0]
    m_ref[...] = jnp.full(m_ref.shape, -jnp.inf, F32)
    l_ref[...] = jnp.zeros(l_ref.shape, F32)
    acc_ref[...] = jnp.zeros(acc_ref.shape, F32)

    def step(j, masked):
        start = pl.multiple_of(j * TK, TK)
        k = jnp.concatenate([kn_ref[0, 0, pl.ds(start, TK), :], kr_ref[0, pl.ds(start, TK), :]], axis=1)
        s = _dot(k, q_t)
        if masked:
            key = lax.broadcasted_iota(jnp.int32, s.shape, 0)
            qry = lax.broadcasted_iota(jnp.int32, s.shape, 1)
            s = jnp.where(key <= qry, s, -jnp.inf)
        m_old = m_ref[...]
        m_new = jnp.maximum(m_old, jnp.max(s, axis=0, keepdims=True))
        alpha = jnp.exp(m_old - m_new)
        p = jnp.exp(s - m_new)
        l_ref[...] = alpha * l_ref[...] + jnp.sum(p, axis=0, keepdims=True)
        acc_ref[...] = alpha * acc_ref[...] + _dot(vt_ref[0, 0, j], p.astype(BF16))
        m_ref[...] = m_new

    def body(j, carry):
        step(j, masked=False)
        return carry

    lax.fori_loop(0, i, body, 0)
    step(i, masked=True)
    out_t = acc_ref[...] / l_ref[...]
    o_ref[0] = out_t.T.astype(o_ref.dtype)


def _attention(q_t, kn, krope, v_t, batch, seq):
    assert TQ == TK
    nq = seq // TQ
    return pl.pallas_call(
        _attention_kernel,
        grid=(batch, HEADS, nq),
        in_specs=[
            pl.BlockSpec((1, 1, QK_PAD, TQ), lambda b, h, i: (b, h, 0, i)),
            pl.BlockSpec((1, 1, seq, NOPE), lambda b, h, i: (b, h, 0, 0)),
            pl.BlockSpec((1, seq, LANES), lambda b, h, i: (b, 0, 0)),
            pl.BlockSpec((1, 1, seq // TK, V_DIM, TK), lambda b, h, i: (b, h, 0, 0, 0)),
        ],
        out_specs=pl.BlockSpec((1, TQ, V_DIM), lambda b, h, i: (b, i, h)),
        out_shape=jax.ShapeDtypeStruct((batch, seq, HEADS * V_DIM), BF16),
        scratch_shapes=[pltpu.VMEM((1, TQ), F32), pltpu.VMEM((1, TQ), F32), pltpu.VMEM((V_DIM, TQ), F32)],
        compiler_params=_params("parallel", "parallel", "arbitrary"),
        name="attention",
    )(q_t, kn, krope, v_t)


def _rglru_kernel(x_ref, g_ref, cw_ref, cb_ref, wa_ref, ba_ref, wx_ref, bx_ref, lam_ref,
                  y_ref, xbuf_ref, a_ref, b_ref, h_ref):
    t = pl.program_id(1)
    ts = x_ref.shape[0]

    @pl.when(t == 0)
    def _():
        xbuf_ref[0:SUBLANES, :] = jnp.zeros((SUBLANES, LRU_WIDTH), F32)
        h_ref[...] = jnp.zeros(h_ref.shape, F32)

    xbuf_ref[SUBLANES:, :] = x_ref[...]
    xc = cb_ref[...] + xbuf_ref[SUBLANES - (CONV_WIDTH - 1):SUBLANES - (CONV_WIDTH - 1) + ts, :] * cw_ref[0:1, :]
    for tap in range(1, CONV_WIDTH):
        lo = SUBLANES - (CONV_WIDTH - 1) + tap
        xc = xc + xbuf_ref[lo:lo + ts, :] * cw_ref[tap:tap + 1, :]
    xbuf_ref[0:SUBLANES, :] = x_ref[ts - SUBLANES:, :]

    xcb = xc.astype(BF16)
    n_grp = LRU_WIDTH // MXU_DIM
    pre_r = jnp.concatenate(
        [_dot(xcb[:, n * MXU_DIM:(n + 1) * MXU_DIM], wa_ref[n]) for n in range(n_grp)], axis=1)
    pre_i = jnp.concatenate(
        [_dot(xcb[:, n * MXU_DIM:(n + 1) * MXU_DIM], wx_ref[n]) for n in range(n_grp)], axis=1)
    r = _sigmoid(pre_r + ba_ref[...])
    gi = _sigmoid(pre_i + bx_ref[...])

    lam = lam_ref[...]
    softplus_neg = jnp.maximum(-lam, 0.0) + jnp.log1p(jnp.exp(-jnp.abs(lam)))
    log_a = (-LRU_C) * r * softplus_neg
    a = jnp.exp(log_a)
    a_ref[...] = a
    b_ref[...] = jnp.sqrt(-jnp.tanh(log_a) * (1.0 + a * a)) * (gi * xc)

    row = lax.broadcasted_iota(jnp.int32, (SUBLANES, LRU_WIDTH), 0)

    def group(gidx, carry):
        r0 = pl.multiple_of(gidx * SUBLANES, SUBLANES)
        a = a_ref[pl.ds(r0, SUBLANES), :]
        b = b_ref[pl.ds(r0, SUBLANES), :]
        for sh in (1, 2, 4):
            keep = row >= sh
            a_prev = jnp.where(keep, pltpu.roll(a, sh, 0), 1.0)
            b_prev = jnp.where(keep, pltpu.roll(b, sh, 0), 0.0)
            b = a * b_prev + b
            a = a * a_prev
        h = a * carry + b
        b_ref[pl.ds(r0, SUBLANES), :] = h
        return h[SUBLANES - 1:SUBLANES, :]

    h_last = lax.fori_loop(0, ts // SUBLANES, group, h_ref[...], unroll=4)
    h_ref[...] = h_last

    g = g_ref[...].astype(F32)
    gelu = 0.5 * g * (1.0 + jnp.tanh(math.sqrt(2.0 / math.pi) * (g + 0.044715 * (g * g * g))))
    y_ref[...] = (b_ref[...] * gelu).astype(y_ref.dtype)


def _rglru(x_lru, g_lru, conv_w, conv_b, wa_blk, ba, wx_blk, bx, lam, batch, seq):
    nblk = seq // TS
    tok = lambda b, t: (b * nblk + t, 0)
    return pl.pallas_call(
        _rglru_kernel,
        grid=(batch, nblk),
        in_specs=[
            pl.BlockSpec((TS, LRU_WIDTH), tok), pl.BlockSpec((TS, LRU_WIDTH), tok),
            _const_spec(conv_w.shape), _const_spec(conv_b.shape),
            _const_spec(wa_blk.shape), _const_spec(ba.shape),
            _const_spec(wx_blk.shape), _const_spec(bx.shape), _const_spec(lam.shape),
        ],
        out_specs=pl.BlockSpec((TS, LRU_WIDTH), tok),
        out_shape=jax.ShapeDtypeStruct((batch * seq, LRU_WIDTH), BF16),
        scratch_shapes=[
            pltpu.VMEM((TS + SUBLANES, LRU_WIDTH), F32),
            pltpu.VMEM((TS, LRU_WIDTH), F32),
            pltpu.VMEM((TS, LRU_WIDTH), F32),
            pltpu.VMEM((1, LRU_WIDTH), F32),
        ],
        compiler_params=_params("parallel", "arbitrary"),
        name="rglru",
    )(x_lru, g_lru, conv_w, conv_b, wa_blk, ba, wx_blk, bx, lam)


def _merge_kernel(x_ref, ya_ref, yb_ref, ga_ref, gb_ref, wpa_ref, wpl_ref, wo_ref, o_ref):
    pa = _dot(ya_ref[...], wpa_ref[...])
    pb = _dot(yb_ref[...], wpl_ref[...])
    merged = _sigmoid(ga_ref[...].astype(F32)) * pa + _sigmoid(gb_ref[...].astype(F32)) * pb
    o_ref[...] = x_ref[...] + _dot(merged.astype(BF16), wo_ref[...])


def _merge(x2d, ya, yb, ga, gb, wpa, wpl, wo):
    m = x2d.shape[0]
    row = lambda i: (i, 0)
    tile = pl.BlockSpec((TM_PROJ, D_MODEL), row)
    return pl.pallas_call(
        _merge_kernel,
        grid=(m // TM_PROJ,),
        in_specs=[tile] * 5 + [_const_spec(wpa.shape), _const_spec(wpl.shape), _const_spec(wo.shape)],
        out_specs=tile,
        out_shape=jax.ShapeDtypeStruct((m, D_MODEL), F32),
        compiler_params=_params("parallel"),
        name="merge",
    )(x2d, ya, yb, ga, gb, wpa, wpl, wo)


def _ffn_kernel(x_ref, g_ref, wg_ref, wu_ref, wd_ref, o_ref):
    x = x_ref[...]
    h = _rms_rows(x, g_ref[...]).astype(BF16)
    gate = _dot(h, wg_ref[...])
    up = _dot(h, wu_ref[...])
    act = (gate * _sigmoid(gate) * up).astype(BF16)
    o_ref[...] = x + _dot(act, wd_ref[...])


def _ffn(x2d, norm_g, wg, wu, wd):
    m = x2d.shape[0]
    tm = TM_PROJ // 2
    row = lambda i: (i, 0)
    tile = pl.BlockSpec((tm, D_MODEL), row)
    return pl.pallas_call(
        _ffn_kernel,
        grid=(m // tm,),
        in_specs=[tile, _const_spec(norm_g.shape), _const_spec(wg.shape), _const_spec(wu.shape),
                  _const_spec(wd.shape)],
        out_specs=tile,
        out_shape=jax.ShapeDtypeStruct((m, D_MODEL), F32),
        compiler_params=_params("parallel"),
        name="ffn",
    )(x2d, norm_g, wg, wu, wd)


def _block_diag_groups(w):
    per = MXU_DIM // LRU_BLOCK
    n_grp = w.shape[0] // per
    w4 = w.reshape(n_grp, per, LRU_BLOCK, LRU_BLOCK)
    eye = jnp.eye(per, dtype=w.dtype)
    out = jnp.einsum("gpde,pq->gpdqe", w4, eye)
    return out.reshape(n_grp, MXU_DIM, MXU_DIM)


def kernel(x, positions, norm_mix_g, w_in, q_lat_g, w_q_up, kv_lat_g, w_kv_up, q_head_g, k_head_g,
           conv_w, conv_b, lru_wa, lru_ba, lru_wx, lru_bx, lru_lambda, w_proj_attn, w_proj_lru,
           w_out, norm_ffn_g, w_ffn_gate, w_ffn_up, w_ffn_down):
    batch, seq, d = x.shape
    depth = w_in.shape[0]
    half = HALF_ROPE
    inv_freq = (ROPE_THETA ** (-jnp.arange(half, dtype=F32) / half)).reshape(half, 1)
    positions3 = positions.reshape(batch, 1, seq)
    x2d = x.reshape(batch * seq, d)

    for l in range(depth):
        w = w_in[l].astype(BF16)
        c0 = Q_RANK
        c1 = c0 + KV_RANK
        c2 = c1 + ROPE
        c3 = c2 + LRU_WIDTH
        c4 = c3 + LRU_WIDTH
        c5 = c4 + D_MODEL
        w_kr = jnp.pad(w[:, c1:c2], ((0, 0), (0, LANES - ROPE)))
        w_parts = [w[:, :c0], w[:, c0:c1], w_kr, w[:, c2:c3], w[:, c3:c4], w[:, c4:c5], w[:, c5:]]
        cq, ckv, kr, x_lru, g_lru, gate_a, gate_b = _in_proj(x2d, norm_mix_g[l].reshape(1, d), w_parts)

        wkv = w_kv_up[l].astype(BF16).reshape(KV_RANK, HEADS, NOPE + V_DIM)
        wkn = wkv[:, :, :NOPE].reshape(KV_RANK, HEADS * NOPE)
        wv_t = wkv[:, :, NOPE:].reshape(KV_RANK, HEADS * V_DIM).T
        wq_t = w_q_up[l].astype(BF16).T
        q_t, kn, krope, v_t = _mla_prep(
            cq, ckv, kr, positions3, inv_freq,
            q_lat_g[l].reshape(1, Q_RANK), kv_lat_g[l].reshape(1, KV_RANK), wq_t, wkn, wv_t,
            q_head_g[l][:NOPE].reshape(NOPE, 1), q_head_g[l][NOPE:].reshape(ROPE, 1),
            k_head_g[l][:NOPE].reshape(1, NOPE), k_head_g[l][NOPE:].reshape(ROPE, 1),
            batch, seq)
        y_a = _attention(q_t, kn, krope, v_t, batch, seq).reshape(batch * seq, HEADS * V_DIM)

        y_b = _rglru(
            x_lru, g_lru, conv_w[l], conv_b[l].reshape(1, LRU_WIDTH),
            _block_diag_groups(lru_wa[l]).astype(BF16), lru_ba[l].reshape(1, LRU_WIDTH),
            _block_diag_groups(lru_wx[l]).astype(BF16), lru_bx[l].reshape(1, LRU_WIDTH),
            lru_lambda[l].reshape(1, LRU_WIDTH), batch, seq)

        x2d = _merge(x2d, y_a, y_b, gate_a, gate_b, w_proj_attn[l].astype(BF16),
                     w_proj_lru[l].astype(BF16), w_out[l].astype(BF16))
        x2d = _ffn(x2d, norm_ffn_g[l].reshape(1, d), w_ffn_gate[l].astype(BF16),
                   w_ffn_up[l].astype(BF16), w_ffn_down[l].astype(BF16))
    return x2d.reshape(batch, seq, d)
```

```python
import functools
import math

import jax
import jax.numpy as jnp
from jax import lax
from jax.experimental import pallas as pl
from jax.experimental.pallas import tpu as pltpu

D_MODEL = 1024
HEADS = 8
NOPE = 128
ROPE = 64
HALF_ROPE = ROPE // 2
QK_DIM = NOPE + ROPE
V_DIM = 128
Q_RANK = 256
KV_RANK = 256
ROPE_THETA = 10000.0
LRU_WIDTH = 1024
LRU_BLOCK = 64
CONV_WIDTH = 4
LRU_C = 8.0
EPS = 1e-6

LANES = 128
SUBLANES = 8
MXU_DIM = 256
QK_PAD = 2 * LANES
SUM_ROWS = 16

LOG2_E = math.log2(math.e)
SHIFT_LIMIT_LOG2 = 60.0

VMEM_LIMIT = 56 * 1024 * 1024

TM_PROJ = 512
TM_PREP = 512
TQ = 512
TK = 512
TS = 256
PAIRS_PER_TRIP = 8

F32 = jnp.float32
BF16 = jnp.bfloat16


def _params(*semantics):
    return pltpu.CompilerParams(dimension_semantics=semantics, vmem_limit_bytes=VMEM_LIMIT)


def _const_spec(shape):
    zeros = (0,) * len(shape)
    return pl.BlockSpec(shape, lambda *_: zeros)


def _sigmoid(v):
    return 1.0 / (1.0 + jnp.exp(-v))


def _rms_rows(v, gain_row):
    ms = jnp.mean(v * v, axis=-1, keepdims=True)
    return v * lax.rsqrt(ms + EPS) * gain_row


def _rms_cols(v, gain_col):
    ms = jnp.mean(v * v, axis=0, keepdims=True)
    return v * lax.rsqrt(ms + EPS) * gain_col


def _dot(a, b):
    return jnp.dot(a, b, preferred_element_type=F32)


def _dot_nt(a, b):
    return lax.dot_general(a, b, (((1,), (1,)), ((), ())), preferred_element_type=F32)


def _in_proj_kernel(x_ref, g_ref, wq_ref, wkv_ref, wkr_ref, wx_ref, wg_ref, wa_ref, wb_ref,
                    cq_ref, ckv_ref, kr_ref, xl_ref, gl_ref, ga_ref, gb_ref):
    h = _rms_rows(x_ref[...], g_ref[...]).astype(BF16)
    cq_ref[...] = _dot(h, wq_ref[...])
    ckv_ref[...] = _dot(h, wkv_ref[...])
    kr_ref[...] = _dot(h, wkr_ref[...])
    xl_ref[...] = _dot(h, wx_ref[...])
    gl_ref[...] = _dot(h, wg_ref[...]).astype(BF16)
    ga_ref[...] = _dot(h, wa_ref[...]).astype(BF16)
    gb_ref[...] = _dot(h, wb_ref[...]).astype(BF16)


def _in_proj(x2d, norm_g, w_parts):
    m = x2d.shape[0]
    widths = [w.shape[1] for w in w_parts]
    dtypes = [F32, F32, F32, F32, BF16, BF16, BF16]
    row = lambda i: (i, 0)
    return pl.pallas_call(
        _in_proj_kernel,
        grid=(m // TM_PROJ,),
        in_specs=[pl.BlockSpec((TM_PROJ, D_MODEL), row), _const_spec((1, D_MODEL))]
        + [_const_spec(w.shape) for w in w_parts],
        out_specs=[pl.BlockSpec((TM_PROJ, n), row) for n in widths],
        out_shape=[jax.ShapeDtypeStruct((m, n), dt) for n, dt in zip(widths, dtypes)],
        compiler_params=_params("parallel"),
        name="in_proj",
    )(x2d, norm_g, *w_parts)


def _mla_prep_kernel(cq_ref, ckv_ref, kr_ref, pos_ref, freq_ref, off_ref, qlg_ref, kvlg_ref,
                     wqt_ref, wkn_ref, wvt_ref, gqn_ref, gqr_ref, gkn_ref, gkr_ref,
                     qt_ref, kn_ref, krope_ref, vt_ref):
    scale = QK_DIM ** -0.5 * LOG2_E
    tm = cq_ref.shape[0]
    pad_row = lax.broadcasted_iota(jnp.int32, (QK_PAD - QK_DIM, tm), 0)
    ang = freq_ref[...] * pos_ref[0].astype(F32)
    cos = jnp.cos(ang)
    sin = jnp.sin(ang)

    def rope_cols(v):
        x1, x2 = v[:HALF_ROPE], v[HALF_ROPE:]
        return x1 * cos - x2 * sin, x2 * cos + x1 * sin

    cqn = _rms_rows(cq_ref[...], qlg_ref[...]).astype(BF16)
    q_t = _dot_nt(wqt_ref[...], cqn)
    pad = jnp.where(pad_row == 0, off_ref[...], 0.0).astype(BF16)
    for h in range(HEADS):
        base = h * QK_DIM
        qn = _rms_cols(q_t[base:base + NOPE], gqn_ref[...]) * scale
        qr = _rms_cols(q_t[base + NOPE:base + QK_DIM], gqr_ref[...]) * scale
        o1, o2 = rope_cols(qr)
        q_pad = jnp.concatenate([qn.astype(BF16), o1.astype(BF16), o2.astype(BF16), pad], axis=0)
        for c in range(qt_ref.shape[2]):
            qt_ref[0, h, c] = q_pad[:, c * TQ:(c + 1) * TQ]

    ckvn = _rms_rows(ckv_ref[...], kvlg_ref[...]).astype(BF16)
    kn_all = _dot(ckvn, wkn_ref[...])
    v_t = _dot_nt(wvt_ref[...], ckvn)
    n_chunks = vt_ref.shape[2]
    for h in range(HEADS):
        kn = _rms_rows(kn_all[:, h * NOPE:(h + 1) * NOPE], gkn_ref[...])
        kn_ref[0, h] = kn.astype(BF16)
        for c in range(n_chunks):
            vt_ref[0, h, c] = v_t[h * V_DIM:(h + 1) * V_DIM, c * TK:(c + 1) * TK].astype(BF16)

    kr_t = kr_ref[...].T
    krn = _rms_cols(kr_t[:ROPE], gkr_ref[...])
    o1, o2 = rope_cols(krn)
    kr_out = jnp.concatenate([o1, o2, jnp.where(pad_row == 0, 1.0, 0.0)], axis=0)
    krope_ref[0] = kr_out.T.astype(BF16)


def _mla_prep(cq, ckv, kr, positions3, inv_freq, offset, q_lat_g, kv_lat_g, wq_t, wkn, wv_t,
              gqn, gqr, gkn, gkr, batch, seq):
    tm = TM_PREP
    nblk = seq // tm
    tok = lambda b, i: (b * nblk + i, 0)
    in_specs = [
        pl.BlockSpec((tm, Q_RANK), tok),
        pl.BlockSpec((tm, KV_RANK), tok),
        pl.BlockSpec((tm, LANES), tok),
        pl.BlockSpec((1, 1, tm), lambda b, i: (b, 0, i)),
        _const_spec(inv_freq.shape), _const_spec(offset.shape),
        _const_spec(q_lat_g.shape), _const_spec(kv_lat_g.shape),
        _const_spec(wq_t.shape), _const_spec(wkn.shape), _const_spec(wv_t.shape),
        _const_spec(gqn.shape), _const_spec(gqr.shape), _const_spec(gkn.shape), _const_spec(gkr.shape),
    ]
    out_specs = [
        pl.BlockSpec((1, HEADS, tm // TQ, QK_PAD, TQ), lambda b, i: (b, 0, i, 0, 0)),
        pl.BlockSpec((1, HEADS, tm, NOPE), lambda b, i: (b, 0, i, 0)),
        pl.BlockSpec((1, tm, LANES), lambda b, i: (b, i, 0)),
        pl.BlockSpec((1, HEADS, tm // TK, V_DIM, TK), lambda b, i: (b, 0, i, 0, 0)),
    ]
    out_shape = [
        jax.ShapeDtypeStruct((batch, HEADS, seq // TQ, QK_PAD, TQ), BF16),
        jax.ShapeDtypeStruct((batch, HEADS, seq, NOPE), BF16),
        jax.ShapeDtypeStruct((batch, seq, LANES), BF16),
        jax.ShapeDtypeStruct((batch, HEADS, seq // TK, V_DIM, TK), BF16),
    ]
    return pl.pallas_call(
        _mla_prep_kernel,
        grid=(batch, nblk),
        in_specs=in_specs,
        out_specs=out_specs,
        out_shape=out_shape,
        compiler_params=_params("parallel", "parallel"),
        name="mla_prep",
    )(cq, ckv, kr, positions3, inv_freq, offset, q_lat_g, kv_lat_g, wq_t, wkn, wv_t, gqn, gqr, gkn, gkr)


def _scores_t(qt_ref, kn_ref, kr_ref, i, j):
    start = pl.multiple_of(j * TK, TK)
    k = jnp.concatenate([kn_ref[0, 0, pl.ds(start, TK), :], kr_ref[0, pl.ds(start, TK), :]], axis=1)
    return _dot(k, qt_ref[0, 0, i])


def _causal_mask(s, i, j):
    key = j * TK + lax.broadcasted_iota(jnp.int32, s.shape, 0)
    qry = i * TQ + lax.broadcasted_iota(jnp.int32, s.shape, 1)
    return jnp.where(key <= qry, s, -jnp.inf)


def _attention_online_kernel(qt_ref, kn_ref, kr_ref, vt_ref, o_ref, m_ref, l_ref, acc_ref):
    i = pl.program_id(2)
    m_ref[...] = jnp.full(m_ref.shape, -jnp.inf, F32)
    l_ref[...] = jnp.zeros(l_ref.shape, F32)
    acc_ref[...] = jnp.zeros(acc_ref.shape, F32)

    def step(j, masked):
        s = _scores_t(qt_ref, kn_ref, kr_ref, 0, j)
        if masked:
            s = _causal_mask(s, i, j)
        m_old = m_ref[...]
        m_new = jnp.maximum(m_old, jnp.max(s, axis=0, keepdims=True))
        alpha = jnp.exp2(m_old - m_new)
        p = jnp.exp2(s - m_new)
        l_ref[...] = alpha * l_ref[...] + jnp.sum(p, axis=0, keepdims=True)
        acc_ref[...] = alpha * acc_ref[...] + _dot(vt_ref[0, 0, j], p.astype(BF16))
        m_ref[...] = m_new

    def body(j, carry):
        step(j, masked=False)
        return carry

    lax.fori_loop(0, i, body, 0)
    step(i, masked=True)
    out_t = acc_ref[...] / l_ref[...]
    o_ref[0] = out_t.T.astype(o_ref.dtype)


def _attention_shifted_kernel(itab_ref, jtab_ref, qt_ref, kn_ref, kr_ref, vt_ref, o_ref,
                              s0_ref, s1_ref, p0_ref, p1_ref, acc_ref, *, n_pairs):
    ones = jnp.ones((SUM_ROWS, TK), BF16)
    s_slots = (s0_ref, s1_ref)
    p_slots = (p0_ref, p1_ref)

    def scores(t, s_ref):
        s_ref[...] = _scores_t(qt_ref, kn_ref, kr_ref, itab_ref[t], jtab_ref[t])

    def probs(t, s_ref, p_ref):
        p_ref[...] = jnp.exp2(_causal_mask(s_ref[...], itab_ref[t], jtab_ref[t])).astype(BF16)

    def values(t, p_ref):
        v_aug = jnp.concatenate([vt_ref[0, 0, jtab_ref[t]], ones], axis=0)
        acc_ref[itab_ref[t]] += _dot(v_aug, p_ref[...])

    acc_ref[...] = jnp.zeros(acc_ref.shape, F32)
    scores(0, s0_ref)
    scores(1, s1_ref)
    probs(0, s0_ref, p0_ref)

    def body(u, carry):
        for sub in range(PAIRS_PER_TRIP):
            t = PAIRS_PER_TRIP * u + sub
            slot = sub % 2
            scores(t + 2, s_slots[slot])
            probs(t + 1, s_slots[1 - slot], p_slots[1 - slot])
            values(t, p_slots[slot])
        return carry

    lax.fori_loop(0, n_pairs // PAIRS_PER_TRIP, body, 0)

    for i in range(acc_ref.shape[0]):
        out_t = acc_ref[i, 0:V_DIM, :] / acc_ref[i, V_DIM:V_DIM + 1, :]
        o_ref[0, i * TQ:(i + 1) * TQ, :] = out_t.T.astype(o_ref.dtype)


def _attention_shifted(q_t, kn, krope, v_t, batch, seq):
    nq = seq // TQ
    pairs = [(i, j) for i in range(nq) for j in range(i + 1)]
    n_pairs = len(pairs)
    assert TQ == TK and n_pairs % PAIRS_PER_TRIP == 0 and PAIRS_PER_TRIP % 2 == 0
    pairs = pairs + [pairs[-1]] * 2
    itab = jnp.asarray([p[0] for p in pairs], jnp.int32)
    jtab = jnp.asarray([p[1] for p in pairs], jnp.int32)
    grid_spec = pltpu.PrefetchScalarGridSpec(
        num_scalar_prefetch=2,
        grid=(batch, HEADS),
        in_specs=[
            pl.BlockSpec((1, 1, nq, QK_PAD, TQ), lambda b, h, *_: (b, h, 0, 0, 0)),
            pl.BlockSpec((1, 1, seq, NOPE), lambda b, h, *_: (b, h, 0, 0)),
            pl.BlockSpec((1, seq, LANES), lambda b, h, *_: (b, 0, 0)),
            pl.BlockSpec((1, 1, seq // TK, V_DIM, TK), lambda b, h, *_: (b, h, 0, 0, 0)),
        ],
        out_specs=pl.BlockSpec((1, seq, V_DIM), lambda b, h, *_: (b, 0, h)),
        scratch_shapes=[
            pltpu.VMEM((TK, TQ), F32), pltpu.VMEM((TK, TQ), F32),
            pltpu.VMEM((TK, TQ), BF16), pltpu.VMEM((TK, TQ), BF16),
            pltpu.VMEM((nq, V_DIM + SUM_ROWS, TQ), F32),
        ],
    )
    return pl.pallas_call(
        functools.partial(_attention_shifted_kernel, n_pairs=n_pairs),
        grid_spec=grid_spec,
        out_shape=jax.ShapeDtypeStruct((batch, seq, HEADS * V_DIM), BF16),
        compiler_params=_params("parallel", "parallel"),
        name="attention_shifted",
    )(itab, jtab, q_t, kn, krope, v_t)


def _attention_online(q_t, kn, krope, v_t, batch, seq):
    assert TQ == TK
    nq = seq // TQ
    return pl.pallas_call(
        _attention_online_kernel,
        grid=(batch, HEADS, nq),
        in_specs=[
            pl.BlockSpec((1, 1, 1, QK_PAD, TQ), lambda b, h, i: (b, h, i, 0, 0)),
            pl.BlockSpec((1, 1, seq, NOPE), lambda b, h, i: (b, h, 0, 0)),
            pl.BlockSpec((1, seq, LANES), lambda b, h, i: (b, 0, 0)),
            pl.BlockSpec((1, 1, seq // TK, V_DIM, TK), lambda b, h, i: (b, h, 0, 0, 0)),
        ],
        out_specs=pl.BlockSpec((1, TQ, V_DIM), lambda b, h, i: (b, i, h)),
        out_shape=jax.ShapeDtypeStruct((batch, seq, HEADS * V_DIM), BF16),
        scratch_shapes=[pltpu.VMEM((1, TQ), F32), pltpu.VMEM((1, TQ), F32), pltpu.VMEM((V_DIM, TQ), F32)],
        compiler_params=_params("parallel", "parallel", "arbitrary"),
        name="attention_online",
    )(q_t, kn, krope, v_t)


def _rglru_kernel(x_ref, g_ref, cw_ref, cb_ref, wa_ref, ba_ref, wx_ref, bx_ref, lam_ref,
                  y_ref, xbuf_ref, a_ref, b_ref, h_ref):
    t = pl.program_id(1)
    ts = x_ref.shape[0]

    @pl.when(t == 0)
    def _():
        xbuf_ref[0:SUBLANES, :] = jnp.zeros((SUBLANES, LRU_WIDTH), F32)
        h_ref[...] = jnp.zeros(h_ref.shape, F32)

    xbuf_ref[SUBLANES:, :] = x_ref[...]
    xc = cb_ref[...] + xbuf_ref[SUBLANES - (CONV_WIDTH - 1):SUBLANES - (CONV_WIDTH - 1) + ts, :] * cw_ref[0:1, :]
    for tap in range(1, CONV_WIDTH):
        lo = SUBLANES - (CONV_WIDTH - 1) + tap
        xc = xc + xbuf_ref[lo:lo + ts, :] * cw_ref[tap:tap + 1, :]
    xbuf_ref[0:SUBLANES, :] = x_ref[ts - SUBLANES:, :]

    xcb = xc.astype(BF16)
    n_grp = LRU_WIDTH // MXU_DIM
    pre_r = jnp.concatenate(
        [_dot(xcb[:, n * MXU_DIM:(n + 1) * MXU_DIM], wa_ref[n]) for n in range(n_grp)], axis=1)
    pre_i = jnp.concatenate(
        [_dot(xcb[:, n * MXU_DIM:(n + 1) * MXU_DIM], wx_ref[n]) for n in range(n_grp)], axis=1)
    r = _sigmoid(pre_r + ba_ref[...])
    gi = _sigmoid(pre_i + bx_ref[...])

    lam = lam_ref[...]
    softplus_neg = jnp.maximum(-lam, 0.0) + jnp.log1p(jnp.exp(-jnp.abs(lam)))
    log_a = (-LRU_C) * r * softplus_neg
    a = jnp.exp(log_a)
    a_ref[...] = a
    b_ref[...] = jnp.sqrt(-jnp.tanh(log_a) * (1.0 + a * a)) * (gi * xc)

    row = lax.broadcasted_iota(jnp.int32, (SUBLANES, LRU_WIDTH), 0)

    def group(gidx, carry):
        r0 = pl.multiple_of(gidx * SUBLANES, SUBLANES)
        a = a_ref[pl.ds(r0, SUBLANES), :]
        b = b_ref[pl.ds(r0, SUBLANES), :]
        for sh in (1, 2, 4):
            keep = row >= sh
            a_prev = jnp.where(keep, pltpu.roll(a, sh, 0), 1.0)
            b_prev = jnp.where(keep, pltpu.roll(b, sh, 0), 0.0)
            b = a * b_prev + b
            a = a * a_prev
        h = a * carry + b
        b_ref[pl.ds(r0, SUBLANES), :] = h
        return h[SUBLANES - 1:SUBLANES, :]

    h_last = lax.fori_loop(0, ts // SUBLANES, group, h_ref[...], unroll=4)
    h_ref[...] = h_last

    g = g_ref[...].astype(F32)
    gelu = 0.5 * g * (1.0 + jnp.tanh(math.sqrt(2.0 / math.pi) * (g + 0.044715 * (g * g * g))))
    y_ref[...] = (b_ref[...] * gelu).astype(y_ref.dtype)


def _rglru(x_lru, g_lru, conv_w, conv_b, wa_blk, ba, wx_blk, bx, lam, batch, seq):
    nblk = seq // TS
    tok = lambda b, t: (b * nblk + t, 0)
    return pl.pallas_call(
        _rglru_kernel,
        grid=(batch, nblk),
        in_specs=[
            pl.BlockSpec((TS, LRU_WIDTH), tok), pl.BlockSpec((TS, LRU_WIDTH), tok),
            _const_spec(conv_w.shape), _const_spec(conv_b.shape),
            _const_spec(wa_blk.shape), _const_spec(ba.shape),
            _const_spec(wx_blk.shape), _const_spec(bx.shape), _const_spec(lam.shape),
        ],
        out_specs=pl.BlockSpec((TS, LRU_WIDTH), tok),
        out_shape=jax.ShapeDtypeStruct((batch * seq, LRU_WIDTH), BF16),
        scratch_shapes=[
            pltpu.VMEM((TS + SUBLANES, LRU_WIDTH), F32),
            pltpu.VMEM((TS, LRU_WIDTH), F32),
            pltpu.VMEM((TS, LRU_WIDTH), F32),
            pltpu.VMEM((1, LRU_WIDTH), F32),
        ],
        compiler_params=_params("parallel", "arbitrary"),
        name="rglru",
    )(x_lru, g_lru, conv_w, conv_b, wa_blk, ba, wx_blk, bx, lam)


def _merge_kernel(x_ref, ya_ref, yb_ref, ga_ref, gb_ref, wpa_ref, wpl_ref, wo_ref, o_ref):
    pa = _dot(ya_ref[...], wpa_ref[...])
    pb = _dot(yb_ref[...], wpl_ref[...])
    merged = _sigmoid(ga_ref[...].astype(F32)) * pa + _sigmoid(gb_ref[...].astype(F32)) * pb
    o_ref[...] = x_ref[...] + _dot(merged.astype(BF16), wo_ref[...])


def _merge(x2d, ya, yb, ga, gb, wpa, wpl, wo):
    m = x2d.shape[0]
    row = lambda i: (i, 0)
    tile = pl.BlockSpec((TM_PROJ, D_MODEL), row)
    return pl.pallas_call(
        _merge_kernel,
        grid=(m // TM_PROJ,),
        in_specs=[tile] * 5 + [_const_spec(wpa.shape), _const_spec(wpl.shape), _const_spec(wo.shape)],
        out_specs=tile,
        out_shape=jax.ShapeDtypeStruct((m, D_MODEL), F32),
        compiler_params=_params("parallel"),
        name="merge",
    )(x2d, ya, yb, ga, gb, wpa, wpl, wo)


def _ffn_kernel(x_ref, g_ref, wg_ref, wu_ref, wd_ref, o_ref):
    x = x_ref[...]
    h = _rms_rows(x, g_ref[...]).astype(BF16)
    gate = _dot(h, wg_ref[...])
    up = _dot(h, wu_ref[...])
    act = (gate * _sigmoid(gate) * up).astype(BF16)
    o_ref[...] = x + _dot(act, wd_ref[...])


def _ffn(x2d, norm_g, wg, wu, wd):
    m = x2d.shape[0]
    tm = TM_PROJ // 2
    row = lambda i: (i, 0)
    tile = pl.BlockSpec((tm, D_MODEL), row)
    return pl.pallas_call(
        _ffn_kernel,
        grid=(m // tm,),
        in_specs=[tile, _const_spec(norm_g.shape), _const_spec(wg.shape), _const_spec(wu.shape),
                  _const_spec(wd.shape)],
        out_specs=tile,
        out_shape=jax.ShapeDtypeStruct((m, D_MODEL), F32),
        compiler_params=_params("parallel"),
        name="ffn",
    )(x2d, norm_g, wg, wu, wd)


def _block_diag_groups(w):
    per = MXU_DIM // LRU_BLOCK
    n_grp = w.shape[0] // per
    w4 = w.reshape(n_grp, per, LRU_BLOCK, LRU_BLOCK)
    eye = jnp.eye(per, dtype=w.dtype)
    out = jnp.einsum("gpde,pq->gpdqe", w4, eye)
    return out.reshape(n_grp, MXU_DIM, MXU_DIM)


def kernel(x, positions, norm_mix_g, w_in, q_lat_g, w_q_up, kv_lat_g, w_kv_up, q_head_g, k_head_g,
           conv_w, conv_b, lru_wa, lru_ba, lru_wx, lru_bx, lru_lambda, w_proj_attn, w_proj_lru,
           w_out, norm_ffn_g, w_ffn_gate, w_ffn_up, w_ffn_down):
    batch, seq, d = x.shape
    depth = w_in.shape[0]
    half = HALF_ROPE
    inv_freq = (ROPE_THETA ** (-jnp.arange(half, dtype=F32) / half)).reshape(half, 1)
    positions3 = positions.reshape(batch, 1, seq)
    x2d = x.reshape(batch * seq, d)

    for l in range(depth):
        w = w_in[l].astype(BF16)
        c0 = Q_RANK
        c1 = c0 + KV_RANK
        c2 = c1 + ROPE
        c3 = c2 + LRU_WIDTH
        c4 = c3 + LRU_WIDTH
        c5 = c4 + D_MODEL
        w_kr = jnp.pad(w[:, c1:c2], ((0, 0), (0, LANES - ROPE)))
        w_parts = [w[:, :c0], w[:, c0:c1], w_kr, w[:, c2:c3], w[:, c3:c4], w[:, c4:c5], w[:, c5:]]
        cq, ckv, kr, x_lru, g_lru, gate_a, gate_b = _in_proj(x2d, norm_mix_g[l].reshape(1, d), w_parts)

        wkv = w_kv_up[l].astype(BF16).reshape(KV_RANK, HEADS, NOPE + V_DIM)
        wkn = wkv[:, :, :NOPE].reshape(KV_RANK, HEADS * NOPE)
        wv_t = wkv[:, :, NOPE:].reshape(KV_RANK, HEADS * V_DIM).T
        wq_t = w_q_up[l].astype(BF16).T
        amax = lambda v: jnp.max(jnp.abs(v))
        gq, gk = q_head_g[l], k_head_g[l]
        bound2 = (QK_DIM ** -0.5 * LOG2_E) * (NOPE * amax(gq[:NOPE]) * amax(gk[:NOPE])
                                              + ROPE * amax(gq[NOPE:]) * amax(gk[NOPE:]))
        shift_ok = bound2 <= SHIFT_LIMIT_LOG2
        offset = jnp.where(shift_ok, -bound2, 0.0).reshape(1, 1)
        q_t, kn, krope, v_t = _mla_prep(
            cq, ckv, kr, positions3, inv_freq, offset,
            q_lat_g[l].reshape(1, Q_RANK), kv_lat_g[l].reshape(1, KV_RANK), wq_t, wkn, wv_t,
            q_head_g[l][:NOPE].reshape(NOPE, 1), q_head_g[l][NOPE:].reshape(ROPE, 1),
            k_head_g[l][:NOPE].reshape(1, NOPE), k_head_g[l][NOPE:].reshape(ROPE, 1),
            batch, seq)
        y_a = lax.cond(
            shift_ok,
            functools.partial(_attention_shifted, batch=batch, seq=seq),
            functools.partial(_attention_online, batch=batch, seq=seq),
            q_t, kn, krope, v_t).reshape(batch * seq, HEADS * V_DIM)

        y_b = _rglru(
            x_lru, g_lru, conv_w[l], conv_b[l].reshape(1, LRU_WIDTH),
            _block_diag_groups(lru_wa[l]).astype(BF16), lru_ba[l].reshape(1, LRU_WIDTH),
            _block_diag_groups(lru_wx[l]).astype(BF16), lru_bx[l].reshape(1, LRU_WIDTH),
            lru_lambda[l].reshape(1, LRU_WIDTH), batch, seq)

        x2d = _merge(x2d, y_a, y_b, gate_a, gate_b, w_proj_attn[l].astype(BF16),
                     w_proj_lru[l].astype(BF16), w_out[l].astype(BF16))
        x2d = _ffn(x2d, norm_ffn_g[l].reshape(1, d), w_ffn_gate[l].astype(BF16),
                   w_ffn_up[l].astype(BF16), w_ffn_down[l].astype(BF16))
    return x2d.reshape(batch, seq, d)
```

```python
import functools
import math

import jax
import jax.numpy as jnp
from jax import lax
from jax.experimental import pallas as pl
from jax.experimental.pallas import tpu as pltpu

D_MODEL = 1024
HEADS = 8
NOPE = 128
ROPE = 64
HALF_ROPE = ROPE // 2
QK_DIM = NOPE + ROPE
V_DIM = 128
Q_RANK = 256
KV_RANK = 256
ROPE_THETA = 10000.0
LRU_WIDTH = 1024
LRU_BLOCK = 64
CONV_WIDTH = 4
LRU_C = 8.0
EPS = 1e-6

LANES = 128
SUBLANES = 8
MXU_DIM = 256
QK_PAD = 2 * LANES
SUM_ROWS = 16

LOG2_E = math.log2(math.e)
SHIFT_LIMIT_LOG2 = 60.0

VMEM_LIMIT = 56 * 1024 * 1024

TM_PROJ = 512
TM_PREP = 512
TQ = 512
TK = 512
LRU_CHUNKS = 16
LRU_CHUNK_LEN = 16
SQRT_FLOOR = 1e-30
PAIRS_PER_TRIP = 34

F32 = jnp.float32
BF16 = jnp.bfloat16


def _params(*semantics):
    return pltpu.CompilerParams(dimension_semantics=semantics, vmem_limit_bytes=VMEM_LIMIT)


def _const_spec(shape):
    zeros = (0,) * len(shape)
    return pl.BlockSpec(shape, lambda *_: zeros)


def _sigmoid(v):
    return 1.0 / (1.0 + jnp.exp(-v))


def _rms_rows(v, gain_row):
    ms = jnp.mean(v * v, axis=-1, keepdims=True)
    return v * lax.rsqrt(ms + EPS) * gain_row


def _rms_cols(v, gain_col):
    ms = jnp.mean(v * v, axis=0, keepdims=True)
    return v * lax.rsqrt(ms + EPS) * gain_col


def _dot(a, b):
    return jnp.dot(a, b, preferred_element_type=F32)


def _dot_nt(a, b):
    return lax.dot_general(a, b, (((1,), (1,)), ((), ())), preferred_element_type=F32)


def _in_proj_kernel(x_ref, g_ref, wq_ref, wkv_ref, wkr_ref, wx_ref, wg_ref, wa_ref, wb_ref,
                    cq_ref, ckv_ref, kr_ref, xl_ref, gl_ref, ga_ref, gb_ref):
    h = _rms_rows(x_ref[...], g_ref[...]).astype(BF16)
    cq_ref[...] = _dot(h, wq_ref[...])
    ckv_ref[...] = _dot(h, wkv_ref[...])
    kr_ref[...] = _dot(h, wkr_ref[...])
    xl_ref[...] = _dot(h, wx_ref[...])
    gl_ref[...] = _dot(h, wg_ref[...]).astype(BF16)
    ga_ref[...] = _dot(h, wa_ref[...]).astype(BF16)
    gb_ref[...] = _dot(h, wb_ref[...]).astype(BF16)


def _in_proj(x2d, norm_g, w_parts):
    m = x2d.shape[0]
    widths = [w.shape[1] for w in w_parts]
    dtypes = [F32, F32, F32, F32, BF16, BF16, BF16]
    row = lambda i: (i, 0)
    return pl.pallas_call(
        _in_proj_kernel,
        grid=(m // TM_PROJ,),
        in_specs=[pl.BlockSpec((TM_PROJ, D_MODEL), row), _const_spec((1, D_MODEL))]
        + [_const_spec(w.shape) for w in w_parts],
        out_specs=[pl.BlockSpec((TM_PROJ, n), row) for n in widths],
        out_shape=[jax.ShapeDtypeStruct((m, n), dt) for n, dt in zip(widths, dtypes)],
        compiler_params=_params("parallel"),
        name="in_proj",
    )(x2d, norm_g, *w_parts)


def _mla_prep_kernel(cq_ref, ckv_ref, kr_ref, pos_ref, freq_ref, off_ref, qlg_ref, kvlg_ref,
                     wqt_ref, wkn_ref, wvt_ref, gqn_ref, gqr_ref, gkn_ref, gkr_ref,
                     qt_ref, kn_ref, krope_ref, vt_ref):
    scale = QK_DIM ** -0.5 * LOG2_E
    tm = cq_ref.shape[0]
    pad_row = lax.broadcasted_iota(jnp.int32, (QK_PAD - QK_DIM, tm), 0)
    ang = freq_ref[...] * pos_ref[0].astype(F32)
    cos = jnp.cos(ang)
    sin = jnp.sin(ang)

    def rope_cols(v):
        x1, x2 = v[:HALF_ROPE], v[HALF_ROPE:]
        return x1 * cos - x2 * sin, x2 * cos + x1 * sin

    cqn = _rms_rows(cq_ref[...], qlg_ref[...]).astype(BF16)
    q_t = _dot_nt(wqt_ref[...], cqn)
    pad = jnp.where(pad_row == 0, off_ref[...], 0.0).astype(BF16)
    for h in range(HEADS):
        base = h * QK_DIM
        qn = _rms_cols(q_t[base:base + NOPE], gqn_ref[...]) * scale
        qr = _rms_cols(q_t[base + NOPE:base + QK_DIM], gqr_ref[...]) * scale
        o1, o2 = rope_cols(qr)
        q_pad = jnp.concatenate([qn.astype(BF16), o1.astype(BF16), o2.astype(BF16), pad], axis=0)
        for c in range(qt_ref.shape[2]):
            qt_ref[0, h, c] = q_pad[:, c * TQ:(c + 1) * TQ]

    ckvn = _rms_rows(ckv_ref[...], kvlg_ref[...]).astype(BF16)
    kn_all = _dot(ckvn, wkn_ref[...])
    v_t = _dot_nt(wvt_ref[...], ckvn)
    n_chunks = vt_ref.shape[2]
    for h in range(HEADS):
        kn = _rms_rows(kn_all[:, h * NOPE:(h + 1) * NOPE], gkn_ref[...])
        kn_ref[0, h] = kn.astype(BF16)
        for c in range(n_chunks):
            vt_ref[0, h, c] = v_t[h * V_DIM:(h + 1) * V_DIM, c * TK:(c + 1) * TK].astype(BF16)

    kr_t = kr_ref[...].T
    krn = _rms_cols(kr_t[:ROPE], gkr_ref[...])
    o1, o2 = rope_cols(krn)
    kr_out = jnp.concatenate([o1, o2, jnp.where(pad_row == 0, 1.0, 0.0)], axis=0)
    krope_ref[0] = kr_out.T.astype(BF16)


def _mla_prep(cq, ckv, kr, positions3, inv_freq, offset, q_lat_g, kv_lat_g, wq_t, wkn, wv_t,
              gqn, gqr, gkn, gkr, batch, seq):
    tm = TM_PREP
    nblk = seq // tm
    tok = lambda b, i: (b * nblk + i, 0)
    in_specs = [
        pl.BlockSpec((tm, Q_RANK), tok),
        pl.BlockSpec((tm, KV_RANK), tok),
        pl.BlockSpec((tm, LANES), tok),
        pl.BlockSpec((1, 1, tm), lambda b, i: (b, 0, i)),
        _const_spec(inv_freq.shape), _const_spec(offset.shape),
        _const_spec(q_lat_g.shape), _const_spec(kv_lat_g.shape),
        _const_spec(wq_t.shape), _const_spec(wkn.shape), _const_spec(wv_t.shape),
        _const_spec(gqn.shape), _const_spec(gqr.shape), _const_spec(gkn.shape), _const_spec(gkr.shape),
    ]
    out_specs = [
        pl.BlockSpec((1, HEADS, tm // TQ, QK_PAD, TQ), lambda b, i: (b, 0, i, 0, 0)),
        pl.BlockSpec((1, HEADS, tm, NOPE), lambda b, i: (b, 0, i, 0)),
        pl.BlockSpec((1, tm, LANES), lambda b, i: (b, i, 0)),
        pl.BlockSpec((1, HEADS, tm // TK, V_DIM, TK), lambda b, i: (b, 0, i, 0, 0)),
    ]
    out_shape = [
        jax.ShapeDtypeStruct((batch, HEADS, seq // TQ, QK_PAD, TQ), BF16),
        jax.ShapeDtypeStruct((batch, HEADS, seq, NOPE), BF16),
        jax.ShapeDtypeStruct((batch, seq, LANES), BF16),
        jax.ShapeDtypeStruct((batch, HEADS, seq // TK, V_DIM, TK), BF16),
    ]
    return pl.pallas_call(
        _mla_prep_kernel,
        grid=(batch, nblk),
        in_specs=in_specs,
        out_specs=out_specs,
        out_shape=out_shape,
        compiler_params=_params("parallel", "parallel"),
        name="mla_prep",
    )(cq, ckv, kr, positions3, inv_freq, offset, q_lat_g, kv_lat_g, wq_t, wkn, wv_t, gqn, gqr, gkn, gkr)


def _scores_t(qt_ref, kn_ref, kr_ref, i, j):
    start = pl.multiple_of(j * TK, TK)
    k = jnp.concatenate([kn_ref[0, 0, pl.ds(start, TK), :], kr_ref[0, pl.ds(start, TK), :]], axis=1)
    return _dot(k, qt_ref[0, 0, i])


def _causal_mask(s, i, j):
    key = j * TK + lax.broadcasted_iota(jnp.int32, s.shape, 0)
    qry = i * TQ + lax.broadcasted_iota(jnp.int32, s.shape, 1)
    return jnp.where(key <= qry, s, -jnp.inf)


def _attention_online_kernel(qt_ref, kn_ref, kr_ref, vt_ref, o_ref, m_ref, l_ref, acc_ref):
    i = pl.program_id(2)
    m_ref[...] = jnp.full(m_ref.shape, -jnp.inf, F32)
    l_ref[...] = jnp.zeros(l_ref.shape, F32)
    acc_ref[...] = jnp.zeros(acc_ref.shape, F32)

    def step(j, masked):
        s = _scores_t(qt_ref, kn_ref, kr_ref, 0, j)
        if masked:
            s = _causal_mask(s, i, j)
        m_old = m_ref[...]
        m_new = jnp.maximum(m_old, jnp.max(s, axis=0, keepdims=True))
        alpha = jnp.exp2(m_old - m_new)
        p = jnp.exp2(s - m_new)
        l_ref[...] = alpha * l_ref[...] + jnp.sum(p, axis=0, keepdims=True)
        acc_ref[...] = alpha * acc_ref[...] + _dot(vt_ref[0, 0, j], p.astype(BF16))
        m_ref[...] = m_new

    def body(j, carry):
        step(j, masked=False)
        return carry

    lax.fori_loop(0, i, body, 0)
    step(i, masked=True)
    out_t = acc_ref[...] / l_ref[...]
    o_ref[0] = out_t.T.astype(o_ref.dtype)


def _attention_shifted_kernel(itab_ref, jtab_ref, qt_ref, kn_ref, kr_ref, vt_ref, o_ref,
                              s0_ref, s1_ref, p0_ref, p1_ref, acc_ref, *, n_pairs):
    ones = jnp.ones((SUM_ROWS, TK), BF16)
    s_slots = (s0_ref, s1_ref)
    p_slots = (p0_ref, p1_ref)

    def scores(t, s_ref):
        s_ref[...] = _scores_t(qt_ref, kn_ref, kr_ref, itab_ref[t], jtab_ref[t])

    def probs(t, s_ref, p_ref):
        p_ref[...] = jnp.exp2(_causal_mask(s_ref[...], itab_ref[t], jtab_ref[t])).astype(BF16)

    def values(t, p_ref):
        v_aug = jnp.concatenate([vt_ref[0, 0, jtab_ref[t]], ones], axis=0)
        acc_ref[itab_ref[t]] += _dot(v_aug, p_ref[...])

    acc_ref[...] = jnp.zeros(acc_ref.shape, F32)
    scores(0, s0_ref)
    scores(1, s1_ref)
    probs(0, s0_ref, p0_ref)

    def body(u, carry):
        for sub in range(PAIRS_PER_TRIP):
            t = PAIRS_PER_TRIP * u + sub
            slot = sub % 2
            scores(t + 2, s_slots[slot])
            probs(t + 1, s_slots[1 - slot], p_slots[1 - slot])
            values(t, p_slots[slot])
        return carry

    lax.fori_loop(0, n_pairs // PAIRS_PER_TRIP, body, 0)

    for i in range(acc_ref.shape[0]):
        out_t = acc_ref[i, 0:V_DIM, :] / acc_ref[i, V_DIM:V_DIM + 1, :]
        o_ref[0, i * TQ:(i + 1) * TQ, :] = out_t.T.astype(o_ref.dtype)


def _attention_shifted(q_t, kn, krope, v_t, batch, seq):
    nq = seq // TQ
    pairs = [(i, j) for i in range(nq) for j in range(i + 1)]
    n_pairs = len(pairs)
    assert TQ == TK and n_pairs % PAIRS_PER_TRIP == 0 and PAIRS_PER_TRIP % 2 == 0
    pairs = pairs + [pairs[-1]] * 2
    itab = jnp.asarray([p[0] for p in pairs], jnp.int32)
    jtab = jnp.asarray([p[1] for p in pairs], jnp.int32)
    grid_spec = pltpu.PrefetchScalarGridSpec(
        num_scalar_prefetch=2,
        grid=(batch, HEADS),
        in_specs=[
            pl.BlockSpec((1, 1, nq, QK_PAD, TQ), lambda b, h, *_: (b, h, 0, 0, 0)),
            pl.BlockSpec((1, 1, seq, NOPE), lambda b, h, *_: (b, h, 0, 0)),
            pl.BlockSpec((1, seq, LANES), lambda b, h, *_: (b, 0, 0)),
            pl.BlockSpec((1, 1, seq // TK, V_DIM, TK), lambda b, h, *_: (b, h, 0, 0, 0)),
        ],
        out_specs=pl.BlockSpec((1, seq, V_DIM), lambda b, h, *_: (b, 0, h)),
        scratch_shapes=[
            pltpu.VMEM((TK, TQ), F32), pltpu.VMEM((TK, TQ), F32),
            pltpu.VMEM((TK, TQ), BF16), pltpu.VMEM((TK, TQ), BF16),
            pltpu.VMEM((nq, V_DIM + SUM_ROWS, TQ), F32),
        ],
    )
    return pl.pallas_call(
        functools.partial(_attention_shifted_kernel, n_pairs=n_pairs),
        grid_spec=grid_spec,
        out_shape=jax.ShapeDtypeStruct((batch, seq, HEADS * V_DIM), BF16),
        compiler_params=_params("parallel", "parallel"),
        name="attention_shifted",
    )(itab, jtab, q_t, kn, krope, v_t)


def _attention_online(q_t, kn, krope, v_t, batch, seq):
    assert TQ == TK
    nq = seq // TQ
    return pl.pallas_call(
        _attention_online_kernel,
        grid=(batch, HEADS, nq),
        in_specs=[
            pl.BlockSpec((1, 1, 1, QK_PAD, TQ), lambda b, h, i: (b, h, i, 0, 0)),
            pl.BlockSpec((1, 1, seq, NOPE), lambda b, h, i: (b, h, 0, 0)),
            pl.BlockSpec((1, seq, LANES), lambda b, h, i: (b, 0, 0)),
            pl.BlockSpec((1, 1, seq // TK, V_DIM, TK), lambda b, h, i: (b, h, 0, 0, 0)),
        ],
        out_specs=pl.BlockSpec((1, TQ, V_DIM), lambda b, h, i: (b, i, h)),
        out_shape=jax.ShapeDtypeStruct((batch, seq, HEADS * V_DIM), BF16),
        scratch_shapes=[pltpu.VMEM((1, TQ), F32), pltpu.VMEM((1, TQ), F32), pltpu.VMEM((V_DIM, TQ), F32)],
        compiler_params=_params("parallel", "parallel", "arbitrary"),
        name="attention_online",
    )(q_t, kn, krope, v_t)


def _rglru_kernel(x_ref, g_ref, cw_ref, cb_ref, wa_ref, ba_ref, wx_ref, bx_ref, lam_ref,
                  y_ref, a_ref, b_ref, tail_ref, h_ref):
    t = pl.program_id(1)
    R, L, W = LRU_CHUNKS, LRU_CHUNK_LEN, LRU_WIDTH

    @pl.when(t == 0)
    def _():
        tail_ref[...] = jnp.zeros(tail_ref.shape, F32)
        h_ref[...] = jnp.zeros(h_ref.shape, F32)

    chunk = lax.broadcasted_iota(jnp.int32, (R, W), 0)

    def step_rows(v, tau):
        return v[tau * R:(tau + 1) * R]

    def from_prev_chunk(v, first):
        return jnp.where(chunk == 0, first, pltpu.roll(v, 1, 0))

    x = jnp.concatenate([x_ref[0, :, tau * W:(tau + 1) * W] for tau in range(L)], axis=0)
    lead = [from_prev_chunk(step_rows(x, L - m), tail_ref[m - 1:m, :]) for m in range(CONV_WIDTH - 1, 0, -1)]
    ext = jnp.concatenate(lead + [x], axis=0)
    for m in range(1, CONV_WIDTH):
        tail_ref[m - 1:m, :] = x[(L - m + 1) * R - 1:(L - m + 1) * R]

    xc = cb_ref[...] + ext[0:L * R] * cw_ref[0:1, :]
    for tap in range(1, CONV_WIDTH):
        xc = xc + ext[tap * R:(tap + L) * R] * cw_ref[tap:tap + 1, :]

    xcb = xc.astype(BF16)
    n_grp = W // MXU_DIM
    pre_r = jnp.concatenate(
        [_dot(xcb[:, n * MXU_DIM:(n + 1) * MXU_DIM], wa_ref[n]) for n in range(n_grp)], axis=1)
    pre_i = jnp.concatenate(
        [_dot(xcb[:, n * MXU_DIM:(n + 1) * MXU_DIM], wx_ref[n]) for n in range(n_grp)], axis=1)
    r = 1.0 / (1.0 + jnp.exp2(pre_r + ba_ref[...]))
    gi = 1.0 / (1.0 + jnp.exp2(pre_i + bx_ref[...]))

    lam = lam_ref[...]
    softplus_neg = jnp.maximum(-lam, 0.0) + jnp.log1p(jnp.exp(-jnp.abs(lam)))
    neg_log_a = r * (LRU_C * softplus_neg)
    a = jnp.exp2(neg_log_a * (-LOG2_E))
    m2 = jnp.tanh(neg_log_a) * (1.0 + a * a)
    mult = m2 * lax.rsqrt(jnp.maximum(m2, SQRT_FLOOR))
    a_ref[...] = a
    b_ref[...] = mult * (gi * xc)

    a_cum = a_ref[0:R, :]
    h_loc = b_ref[0:R, :]
    for tau in range(1, L):
        a_t = a_ref[tau * R:(tau + 1) * R, :]
        h_loc = a_t * h_loc + b_ref[tau * R:(tau + 1) * R, :]
        a_cum = a_t * a_cum
        a_ref[tau * R:(tau + 1) * R, :] = a_cum
        b_ref[tau * R:(tau + 1) * R, :] = h_loc

    carry = h_ref[...]
    sh = 1
    while sh < R:
        keep = chunk >= sh
        a_prev = jnp.where(keep, pltpu.roll(a_cum, sh, 0), 1.0)
        h_prev = jnp.where(keep, pltpu.roll(h_loc, sh, 0), 0.0)
        h_loc = a_cum * h_prev + h_loc
        a_cum = a_cum * a_prev
        sh *= 2
    h_end = a_cum * carry + h_loc
    h_in = from_prev_chunk(h_end, carry)
    h_ref[...] = h_end[R - 1:R, :]

    c0 = math.sqrt(2.0 / math.pi)
    for tau in range(L):
        h = b_ref[tau * R:(tau + 1) * R, :] + a_ref[tau * R:(tau + 1) * R, :] * h_in
        g = g_ref[0, :, tau * W:(tau + 1) * W].astype(F32)
        half_g = 0.5 * g
        gelu = half_g + half_g * jnp.tanh(g * (c0 + (c0 * 0.044715) * (g * g)))
        y_ref[0, :, tau * W:(tau + 1) * W] = (h * gelu).astype(y_ref.dtype)


def _rglru(x_lru, g_lru, conv_w, conv_b, wa_blk, ba, wx_blk, bx, lam, batch, seq):
    ts = LRU_CHUNKS * LRU_CHUNK_LEN
    nblk = seq // ts
    chunked = lambda v: v.reshape(batch * nblk, LRU_CHUNKS, LRU_CHUNK_LEN * LRU_WIDTH)
    tile = pl.BlockSpec((1, LRU_CHUNKS, LRU_CHUNK_LEN * LRU_WIDTH), lambda b, t: (b * nblk + t, 0, 0))
    y = pl.pallas_call(
        _rglru_kernel,
        grid=(batch, nblk),
        in_specs=[
            tile, tile,
            _const_spec(conv_w.shape), _const_spec(conv_b.shape),
            _const_spec(wa_blk.shape), _const_spec(ba.shape),
            _const_spec(wx_blk.shape), _const_spec(bx.shape), _const_spec(lam.shape),
        ],
        out_specs=tile,
        out_shape=jax.ShapeDtypeStruct((batch * nblk, LRU_CHUNKS, LRU_CHUNK_LEN * LRU_WIDTH), BF16),
        scratch_shapes=[
            pltpu.VMEM((ts, LRU_WIDTH), F32),
            pltpu.VMEM((ts, LRU_WIDTH), F32),
            pltpu.VMEM((CONV_WIDTH - 1, LRU_WIDTH), F32),
            pltpu.VMEM((1, LRU_WIDTH), F32),
        ],
        compiler_params=_params("parallel", "arbitrary"),
        name="rglru",
    )(chunked(x_lru), chunked(g_lru), conv_w, conv_b, wa_blk, ba, wx_blk, bx, lam)
    return y.reshape(batch * seq, LRU_WIDTH)


def _merge_kernel(x_ref, ya_ref, yb_ref, ga_ref, gb_ref, wpa_ref, wpl_ref, wo_ref, o_ref):
    pa = _dot(ya_ref[...], wpa_ref[...])
    pb = _dot(yb_ref[...], wpl_ref[...])
    merged = _sigmoid(ga_ref[...].astype(F32)) * pa + _sigmoid(gb_ref[...].astype(F32)) * pb
    o_ref[...] = x_ref[...] + _dot(merged.astype(BF16), wo_ref[...])


def _merge(x2d, ya, yb, ga, gb, wpa, wpl, wo):
    m = x2d.shape[0]
    row = lambda i: (i, 0)
    tile = pl.BlockSpec((TM_PROJ, D_MODEL), row)
    return pl.pallas_call(
        _merge_kernel,
        grid=(m // TM_PROJ,),
        in_specs=[tile] * 5 + [_const_spec(wpa.shape), _const_spec(wpl.shape), _const_spec(wo.shape)],
        out_specs=tile,
        out_shape=jax.ShapeDtypeStruct((m, D_MODEL), F32),
        compiler_params=_params("parallel"),
        name="merge",
    )(x2d, ya, yb, ga, gb, wpa, wpl, wo)


def _ffn_kernel(x_ref, g_ref, wg_ref, wu_ref, wd_ref, o_ref):
    x = x_ref[...]
    h = _rms_rows(x, g_ref[...]).astype(BF16)
    gate = _dot(h, wg_ref[...])
    up = _dot(h, wu_ref[...])
    act = (gate * _sigmoid(gate) * up).astype(BF16)
    o_ref[...] = x + _dot(act, wd_ref[...])


def _ffn(x2d, norm_g, wg, wu, wd):
    m = x2d.shape[0]
    tm = TM_PROJ // 2
    row = lambda i: (i, 0)
    tile = pl.BlockSpec((tm, D_MODEL), row)
    return pl.pallas_call(
        _ffn_kernel,
        grid=(m // tm,),
        in_specs=[tile, _const_spec(norm_g.shape), _const_spec(wg.shape), _const_spec(wu.shape),
                  _const_spec(wd.shape)],
        out_specs=tile,
        out_shape=jax.ShapeDtypeStruct((m, D_MODEL), F32),
        compiler_params=_params("parallel"),
        name="ffn",
    )(x2d, norm_g, wg, wu, wd)


def _block_diag_groups(w):
    per = MXU_DIM // LRU_BLOCK
    n_grp = w.shape[0] // per
    w4 = w.reshape(n_grp, per, LRU_BLOCK, LRU_BLOCK)
    eye = jnp.eye(per, dtype=w.dtype)
    out = jnp.einsum("gpde,pq->gpdqe", w4, eye)
    return out.reshape(n_grp, MXU_DIM, MXU_DIM)


def kernel(x, positions, norm_mix_g, w_in, q_lat_g, w_q_up, kv_lat_g, w_kv_up, q_head_g, k_head_g,
           conv_w, conv_b, lru_wa, lru_ba, lru_wx, lru_bx, lru_lambda, w_proj_attn, w_proj_lru,
           w_out, norm_ffn_g, w_ffn_gate, w_ffn_up, w_ffn_down):
    batch, seq, d = x.shape
    depth = w_in.shape[0]
    half = HALF_ROPE
    inv_freq = (ROPE_THETA ** (-jnp.arange(half, dtype=F32) / half)).reshape(half, 1)
    positions3 = positions.reshape(batch, 1, seq)
    x2d = x.reshape(batch * seq, d)

    for l in range(depth):
        w = w_in[l].astype(BF16)
        c0 = Q_RANK
        c1 = c0 + KV_RANK
        c2 = c1 + ROPE
        c3 = c2 + LRU_WIDTH
        c4 = c3 + LRU_WIDTH
        c5 = c4 + D_MODEL
        w_kr = jnp.pad(w[:, c1:c2], ((0, 0), (0, LANES - ROPE)))
        w_parts = [w[:, :c0], w[:, c0:c1], w_kr, w[:, c2:c3], w[:, c3:c4], w[:, c4:c5], w[:, c5:]]
        cq, ckv, kr, x_lru, g_lru, gate_a, gate_b = _in_proj(x2d, norm_mix_g[l].reshape(1, d), w_parts)

        wkv = w_kv_up[l].astype(BF16).reshape(KV_RANK, HEADS, NOPE + V_DIM)
        wkn = wkv[:, :, :NOPE].reshape(KV_RANK, HEADS * NOPE)
        wv_t = wkv[:, :, NOPE:].reshape(KV_RANK, HEADS * V_DIM).T
        wq_t = w_q_up[l].astype(BF16).T
        amax = lambda v: jnp.max(jnp.abs(v))
        gq, gk = q_head_g[l], k_head_g[l]
        bound2 = (QK_DIM ** -0.5 * LOG2_E) * (NOPE * amax(gq[:NOPE]) * amax(gk[:NOPE])
                                              + ROPE * amax(gq[NOPE:]) * amax(gk[NOPE:]))
        shift_ok = bound2 <= SHIFT_LIMIT_LOG2
        offset = jnp.where(shift_ok, -bound2, 0.0).reshape(1, 1)
        q_t, kn, krope, v_t = _mla_prep(
            cq, ckv, kr, positions3, inv_freq, offset,
            q_lat_g[l].reshape(1, Q_RANK), kv_lat_g[l].reshape(1, KV_RANK), wq_t, wkn, wv_t,
            q_head_g[l][:NOPE].reshape(NOPE, 1), q_head_g[l][NOPE:].reshape(ROPE, 1),
            k_head_g[l][:NOPE].reshape(1, NOPE), k_head_g[l][NOPE:].reshape(ROPE, 1),
            batch, seq)
        y_a = lax.cond(
            shift_ok,
            functools.partial(_attention_shifted, batch=batch, seq=seq),
            functools.partial(_attention_online, batch=batch, seq=seq),
            q_t, kn, krope, v_t).reshape(batch * seq, HEADS * V_DIM)

        y_b = _rglru(
            x_lru, g_lru, conv_w[l], conv_b[l].reshape(1, LRU_WIDTH),
            _block_diag_groups(lru_wa[l] * -LOG2_E).astype(BF16), (lru_ba[l] * -LOG2_E).reshape(1, LRU_WIDTH),
            _block_diag_groups(lru_wx[l] * -LOG2_E).astype(BF16), (lru_bx[l] * -LOG2_E).reshape(1, LRU_WIDTH),
            lru_lambda[l].reshape(1, LRU_WIDTH), batch, seq)

        x2d = _merge(x2d, y_a, y_b, gate_a, gate_b, w_proj_attn[l].astype(BF16),
                     w_proj_lru[l].astype(BF16), w_out[l].astype(BF16))
        x2d = _ffn(x2d, norm_ffn_g[l].reshape(1, d), w_ffn_gate[l].astype(BF16),
                   w_ffn_up[l].astype(BF16), w_ffn_down[l].astype(BF16))
    return x2d.reshape(batch, seq, d)
```

```python
import functools
import math

import jax
import jax.numpy as jnp
from jax import lax
from jax.experimental import pallas as pl
from jax.experimental.pallas import tpu as pltpu

D_MODEL = 1024
HEADS = 8
NOPE = 128
ROPE = 64
HALF_ROPE = ROPE // 2
QK_DIM = NOPE + ROPE
V_DIM = 128
Q_RANK = 256
KV_RANK = 256
ROPE_THETA = 10000.0
LRU_WIDTH = 1024
LRU_BLOCK = 64
CONV_WIDTH = 4
LRU_C = 8.0
EPS = 1e-6

LANES = 128
SUBLANES = 8
MXU_DIM = 256
QK_PAD = 2 * LANES
SUM_ROWS = 16

LOG2_E = math.log2(math.e)
SHIFT_LIMIT_LOG2 = 60.0

VMEM_LIMIT = 56 * 1024 * 1024

TM_PROJ = 512
TM_PREP = 512
TQ = 512
TK = 512
LRU_CHUNKS = 16
LRU_CHUNK_LEN = 16
SQRT_FLOOR = 1e-30
PAIRS_PER_TRIP = 34

F32 = jnp.float32
BF16 = jnp.bfloat16


def _params(*semantics):
    return pltpu.CompilerParams(dimension_semantics=semantics, vmem_limit_bytes=VMEM_LIMIT)


def _const_spec(shape):
    zeros = (0,) * len(shape)
    return pl.BlockSpec(shape, lambda *_: zeros)


def _sigmoid(v):
    return 1.0 / (1.0 + jnp.exp(-v))


def _rms_rows(v, gain_row):
    ms = jnp.mean(v * v, axis=-1, keepdims=True)
    return v * lax.rsqrt(ms + EPS) * gain_row


def _rms_cols(v, gain_col):
    ms = jnp.mean(v * v, axis=0, keepdims=True)
    return v * lax.rsqrt(ms + EPS) * gain_col


def _dot(a, b):
    return jnp.dot(a, b, preferred_element_type=F32)


def _dot_nt(a, b):
    return lax.dot_general(a, b, (((1,), (1,)), ((), ())), preferred_element_type=F32)


def _in_proj_kernel(x_ref, g_ref, wq_ref, wkv_ref, wkr_ref, wx_ref, wg_ref, wa_ref, wb_ref,
                    cq_ref, ckv_ref, kr_ref, xl_ref, gl_ref, ga_ref, gb_ref):
    h = _rms_rows(x_ref[...], g_ref[...]).astype(BF16)
    cq_ref[...] = _dot(h, wq_ref[...])
    ckv_ref[...] = _dot(h, wkv_ref[...])
    kr_ref[...] = _dot(h, wkr_ref[...])
    _store_slabs(xl_ref, _dot(h, wx_ref[...]))
    _store_slabs(gl_ref, _dot(h, wg_ref[...]))
    ga_ref[...] = _dot(h, wa_ref[...]).astype(BF16)
    gb_ref[...] = _dot(h, wb_ref[...]).astype(BF16)


def _store_slabs(ref, v):
    for j in range(ref.shape[0]):
        ref[j] = v[:, j * LANES:(j + 1) * LANES].astype(ref.dtype)


def _load_slabs(ref):
    return jnp.concatenate([ref[j] for j in range(ref.shape[0])], axis=1)


def _in_proj(x2d, norm_g, w_parts):
    m = x2d.shape[0]
    row = lambda i: (i, 0)
    slab = lambda i: (0, i, 0)
    n_slab = LRU_WIDTH // LANES
    plain = lambda n, dt: (pl.BlockSpec((TM_PROJ, n), row), jax.ShapeDtypeStruct((m, n), dt))
    slabs = (pl.BlockSpec((n_slab, TM_PROJ, LANES), slab), jax.ShapeDtypeStruct((n_slab, m, LANES), F32))
    outs = [plain(Q_RANK, F32), plain(KV_RANK, F32), plain(LANES, F32), slabs, slabs,
            plain(D_MODEL, BF16), plain(D_MODEL, BF16)]
    return pl.pallas_call(
        _in_proj_kernel,
        grid=(m // TM_PROJ,),
        in_specs=[pl.BlockSpec((TM_PROJ, D_MODEL), row), _const_spec((1, D_MODEL))]
        + [_const_spec(w.shape) for w in w_parts],
        out_specs=[o[0] for o in outs],
        out_shape=[o[1] for o in outs],
        compiler_params=_params("parallel"),
        name="in_proj",
    )(x2d, norm_g, *w_parts)


def _mla_prep_kernel(cq_ref, ckv_ref, kr_ref, pos_ref, freq_ref, off_ref, qlg_ref, kvlg_ref,
                     wqt_ref, wkn_ref, wvt_ref, gqn_ref, gqr_ref, gkn_ref, gkr_ref,
                     qt_ref, kn_ref, krope_ref, vt_ref):
    scale = QK_DIM ** -0.5 * LOG2_E
    tm = cq_ref.shape[0]
    pad_row = lax.broadcasted_iota(jnp.int32, (QK_PAD - QK_DIM, tm), 0)
    ang = freq_ref[...] * pos_ref[0].astype(F32)
    cos = jnp.cos(ang)
    sin = jnp.sin(ang)

    def rope_cols(v):
        x1, x2 = v[:HALF_ROPE], v[HALF_ROPE:]
        return x1 * cos - x2 * sin, x2 * cos + x1 * sin

    cqn = _rms_rows(cq_ref[...], qlg_ref[...]).astype(BF16)
    q_t = _dot_nt(wqt_ref[...], cqn)
    pad = jnp.where(pad_row == 0, off_ref[...], 0.0).astype(BF16)
    for h in range(HEADS):
        base = h * QK_DIM
        qn = _rms_cols(q_t[base:base + NOPE], gqn_ref[...]) * scale
        qr = _rms_cols(q_t[base + NOPE:base + QK_DIM], gqr_ref[...]) * scale
        o1, o2 = rope_cols(qr)
        q_pad = jnp.concatenate([qn.astype(BF16), o1.astype(BF16), o2.astype(BF16), pad], axis=0)
        for c in range(qt_ref.shape[2]):
            qt_ref[0, h, c] = q_pad[:, c * TQ:(c + 1) * TQ]

    ckvn = _rms_rows(ckv_ref[...], kvlg_ref[...]).astype(BF16)
    kn_all = _dot(ckvn, wkn_ref[...])
    v_t = _dot_nt(wvt_ref[...], ckvn)
    n_chunks = vt_ref.shape[2]
    for h in range(HEADS):
        kn = _rms_rows(kn_all[:, h * NOPE:(h + 1) * NOPE], gkn_ref[...])
        kn_ref[0, h] = kn.astype(BF16)
        for c in range(n_chunks):
            vt_ref[0, h, c] = v_t[h * V_DIM:(h + 1) * V_DIM, c * TK:(c + 1) * TK].astype(BF16)

    kr_t = kr_ref[...].T
    krn = _rms_cols(kr_t[:ROPE], gkr_ref[...])
    o1, o2 = rope_cols(krn)
    kr_out = jnp.concatenate([o1, o2, jnp.where(pad_row == 0, 1.0, 0.0)], axis=0)
    krope_ref[0] = kr_out.T.astype(BF16)


def _mla_prep(cq, ckv, kr, positions3, inv_freq, offset, q_lat_g, kv_lat_g, wq_t, wkn, wv_t,
              gqn, gqr, gkn, gkr, batch, seq):
    tm = TM_PREP
    nblk = seq // tm
    tok = lambda b, i: (b * nblk + i, 0)
    in_specs = [
        pl.BlockSpec((tm, Q_RANK), tok),
        pl.BlockSpec((tm, KV_RANK), tok),
        pl.BlockSpec((tm, LANES), tok),
        pl.BlockSpec((1, 1, tm), lambda b, i: (b, 0, i)),
        _const_spec(inv_freq.shape), _const_spec(offset.shape),
        _const_spec(q_lat_g.shape), _const_spec(kv_lat_g.shape),
        _const_spec(wq_t.shape), _const_spec(wkn.shape), _const_spec(wv_t.shape),
        _const_spec(gqn.shape), _const_spec(gqr.shape), _const_spec(gkn.shape), _const_spec(gkr.shape),
    ]
    out_specs = [
        pl.BlockSpec((1, HEADS, tm // TQ, QK_PAD, TQ), lambda b, i: (b, 0, i, 0, 0)),
        pl.BlockSpec((1, HEADS, tm, NOPE), lambda b, i: (b, 0, i, 0)),
        pl.BlockSpec((1, tm, LANES), lambda b, i: (b, i, 0)),
        pl.BlockSpec((1, HEADS, tm // TK, V_DIM, TK), lambda b, i: (b, 0, i, 0, 0)),
    ]
    out_shape = [
        jax.ShapeDtypeStruct((batch, HEADS, seq // TQ, QK_PAD, TQ), BF16),
        jax.ShapeDtypeStruct((batch, HEADS, seq, NOPE), BF16),
        jax.ShapeDtypeStruct((batch, seq, LANES), BF16),
        jax.ShapeDtypeStruct((batch, HEADS, seq // TK, V_DIM, TK), BF16),
    ]
    return pl.pallas_call(
        _mla_prep_kernel,
        grid=(batch, nblk),
        in_specs=in_specs,
        out_specs=out_specs,
        out_shape=out_shape,
        compiler_params=_params("parallel", "parallel"),
        name="mla_prep",
    )(cq, ckv, kr, positions3, inv_freq, offset, q_lat_g, kv_lat_g, wq_t, wkn, wv_t, gqn, gqr, gkn, gkr)


def _scores_t(qt_ref, kn_ref, kr_ref, i, j):
    start = pl.multiple_of(j * TK, TK)
    k = jnp.concatenate([kn_ref[0, 0, pl.ds(start, TK), :], kr_ref[0, pl.ds(start, TK), :]], axis=1)
    return _dot(k, qt_ref[0, 0, i])


def _causal_mask(s, i, j):
    key = j * TK + lax.broadcasted_iota(jnp.int32, s.shape, 0)
    qry = i * TQ + lax.broadcasted_iota(jnp.int32, s.shape, 1)
    return jnp.where(key <= qry, s, -jnp.inf)


def _attention_online_kernel(qt_ref, kn_ref, kr_ref, vt_ref, o_ref, m_ref, l_ref, acc_ref):
    i = pl.program_id(2)
    m_ref[...] = jnp.full(m_ref.shape, -jnp.inf, F32)
    l_ref[...] = jnp.zeros(l_ref.shape, F32)
    acc_ref[...] = jnp.zeros(acc_ref.shape, F32)

    def step(j, masked):
        s = _scores_t(qt_ref, kn_ref, kr_ref, 0, j)
        if masked:
            s = _causal_mask(s, i, j)
        m_old = m_ref[...]
        m_new = jnp.maximum(m_old, jnp.max(s, axis=0, keepdims=True))
        alpha = jnp.exp2(m_old - m_new)
        p = jnp.exp2(s - m_new)
        l_ref[...] = alpha * l_ref[...] + jnp.sum(p, axis=0, keepdims=True)
        acc_ref[...] = alpha * acc_ref[...] + _dot(vt_ref[0, 0, j], p.astype(BF16))
        m_ref[...] = m_new

    def body(j, carry):
        step(j, masked=False)
        return carry

    lax.fori_loop(0, i, body, 0)
    step(i, masked=True)
    out_t = acc_ref[...] / l_ref[...]
    o_ref[0] = out_t.T.astype(o_ref.dtype)


def _attention_shifted_kernel(itab_ref, jtab_ref, qt_ref, kn_ref, kr_ref, vt_ref, o_ref,
                              s0_ref, s1_ref, p0_ref, p1_ref, acc_ref, *, n_pairs):
    ones = jnp.ones((SUM_ROWS, TK), BF16)
    s_slots = (s0_ref, s1_ref)
    p_slots = (p0_ref, p1_ref)

    def scores(t, s_ref):
        s_ref[...] = _scores_t(qt_ref, kn_ref, kr_ref, itab_ref[t], jtab_ref[t])

    def probs(t, s_ref, p_ref):
        p_ref[...] = jnp.exp2(_causal_mask(s_ref[...], itab_ref[t], jtab_ref[t])).astype(BF16)

    def values(t, p_ref):
        v_aug = jnp.concatenate([vt_ref[0, 0, jtab_ref[t]], ones], axis=0)
        acc_ref[itab_ref[t]] += _dot(v_aug, p_ref[...])

    acc_ref[...] = jnp.zeros(acc_ref.shape, F32)
    scores(0, s0_ref)
    scores(1, s1_ref)
    probs(0, s0_ref, p0_ref)

    def body(u, carry):
        for sub in range(PAIRS_PER_TRIP):
            t = PAIRS_PER_TRIP * u + sub
            slot = sub % 2
            scores(t + 2, s_slots[slot])
            probs(t + 1, s_slots[1 - slot], p_slots[1 - slot])
            values(t, p_slots[slot])
        return carry

    lax.fori_loop(0, n_pairs // PAIRS_PER_TRIP, body, 0)

    for i in range(acc_ref.shape[0]):
        out_t = acc_ref[i, 0:V_DIM, :] / acc_ref[i, V_DIM:V_DIM + 1, :]
        o_ref[0, i * TQ:(i + 1) * TQ, :] = out_t.T.astype(o_ref.dtype)


def _attention_shifted(q_t, kn, krope, v_t, batch, seq):
    nq = seq // TQ
    pairs = [(i, j) for i in range(nq) for j in range(i + 1)]
    n_pairs = len(pairs)
    assert TQ == TK and n_pairs % PAIRS_PER_TRIP == 0 and PAIRS_PER_TRIP % 2 == 0
    pairs = pairs + [pairs[-1]] * 2
    itab = jnp.asarray([p[0] for p in pairs], jnp.int32)
    jtab = jnp.asarray([p[1] for p in pairs], jnp.int32)
    grid_spec = pltpu.PrefetchScalarGridSpec(
        num_scalar_prefetch=2,
        grid=(batch, HEADS),
        in_specs=[
            pl.BlockSpec((1, 1, nq, QK_PAD, TQ), lambda b, h, *_: (b, h, 0, 0, 0)),
            pl.BlockSpec((1, 1, seq, NOPE), lambda b, h, *_: (b, h, 0, 0)),
            pl.BlockSpec((1, seq, LANES), lambda b, h, *_: (b, 0, 0)),
            pl.BlockSpec((1, 1, seq // TK, V_DIM, TK), lambda b, h, *_: (b, h, 0, 0, 0)),
        ],
        out_specs=pl.BlockSpec((1, seq, V_DIM), lambda b, h, *_: (b, 0, h)),
        scratch_shapes=[
            pltpu.VMEM((TK, TQ), F32), pltpu.VMEM((TK, TQ), F32),
            pltpu.VMEM((TK, TQ), BF16), pltpu.VMEM((TK, TQ), BF16),
            pltpu.VMEM((nq, V_DIM + SUM_ROWS, TQ), F32),
        ],
    )
    return pl.pallas_call(
        functools.partial(_attention_shifted_kernel, n_pairs=n_pairs),
        grid_spec=grid_spec,
        out_shape=jax.ShapeDtypeStruct((batch, seq, HEADS * V_DIM), BF16),
        compiler_params=_params("parallel", "parallel"),
        name="attention_shifted",
    )(itab, jtab, q_t, kn, krope, v_t)


def _attention_online(q_t, kn, krope, v_t, batch, seq):
    assert TQ == TK
    nq = seq // TQ
    return pl.pallas_call(
        _attention_online_kernel,
        grid=(batch, HEADS, nq),
        in_specs=[
            pl.BlockSpec((1, 1, 1, QK_PAD, TQ), lambda b, h, i: (b, h, i, 0, 0)),
            pl.BlockSpec((1, 1, seq, NOPE), lambda b, h, i: (b, h, 0, 0)),
            pl.BlockSpec((1, seq, LANES), lambda b, h, i: (b, 0, 0)),
            pl.BlockSpec((1, 1, seq // TK, V_DIM, TK), lambda b, h, i: (b, h, 0, 0, 0)),
        ],
        out_specs=pl.BlockSpec((1, TQ, V_DIM), lambda b, h, i: (b, i, h)),
        out_shape=jax.ShapeDtypeStruct((batch, seq, HEADS * V_DIM), BF16),
        scratch_shapes=[pltpu.VMEM((1, TQ), F32), pltpu.VMEM((1, TQ), F32), pltpu.VMEM((V_DIM, TQ), F32)],
        compiler_params=_params("parallel", "parallel", "arbitrary"),
        name="attention_online",
    )(q_t, kn, krope, v_t)


def _rglru_kernel(x_ref, g_ref, cw_ref, cb_ref, wa_ref, ba_ref, wx_ref, bx_ref, lam_ref,
                  y_ref, a_ref, b_ref, tail_ref, h_ref):
    t = pl.program_id(1)
    R, L, W = LRU_CHUNKS, LRU_CHUNK_LEN, LRU_WIDTH

    @pl.when(t == 0)
    def _():
        tail_ref[...] = jnp.zeros(tail_ref.shape, F32)
        h_ref[...] = jnp.zeros(h_ref.shape, F32)

    chunk = lax.broadcasted_iota(jnp.int32, (R, W), 0)

    def chunk_rows(ref, tau):
        return jnp.concatenate(
            [ref[j, pl.ds(tau, R, stride=L), :] for j in range(ref.shape[0])], axis=1)

    def step_rows(v, tau):
        return v[tau * R:(tau + 1) * R]

    def from_prev_chunk(v, first):
        return jnp.where(chunk == 0, first, pltpu.roll(v, 1, 0))

    x = jnp.concatenate([chunk_rows(x_ref, tau) for tau in range(L)], axis=0)
    lead = [from_prev_chunk(step_rows(x, L - m), tail_ref[m - 1:m, :]) for m in range(CONV_WIDTH - 1, 0, -1)]
    ext = jnp.concatenate(lead + [x], axis=0)
    for m in range(1, CONV_WIDTH):
        tail_ref[m - 1:m, :] = x[(L - m + 1) * R - 1:(L - m + 1) * R]

    xc = cb_ref[...] + ext[0:L * R] * cw_ref[0:1, :]
    for tap in range(1, CONV_WIDTH):
        xc = xc + ext[tap * R:(tap + L) * R] * cw_ref[tap:tap + 1, :]

    xcb = xc.astype(BF16)
    n_grp = W // MXU_DIM
    pre_r = jnp.concatenate(
        [_dot(xcb[:, n * MXU_DIM:(n + 1) * MXU_DIM], wa_ref[n]) for n in range(n_grp)], axis=1)
    pre_i = jnp.concatenate(
        [_dot(xcb[:, n * MXU_DIM:(n + 1) * MXU_DIM], wx_ref[n]) for n in range(n_grp)], axis=1)
    r = 1.0 / (1.0 + jnp.exp2(pre_r + ba_ref[...]))
    gi = 1.0 / (1.0 + jnp.exp2(pre_i + bx_ref[...]))

    lam = lam_ref[...]
    softplus_neg = jnp.maximum(-lam, 0.0) + jnp.log1p(jnp.exp(-jnp.abs(lam)))
    neg_log_a = r * (LRU_C * softplus_neg)
    a = jnp.exp2(neg_log_a * (-LOG2_E))
    m2 = jnp.tanh(neg_log_a) * (1.0 + a * a)
    mult = m2 * lax.rsqrt(jnp.maximum(m2, SQRT_FLOOR))
    a_ref[...] = a
    b_ref[...] = mult * (gi * xc)

    a_cum = a_ref[0:R, :]
    h_loc = b_ref[0:R, :]
    for tau in range(1, L):
        a_t = a_ref[tau * R:(tau + 1) * R, :]
        h_loc = a_t * h_loc + b_ref[tau * R:(tau + 1) * R, :]
        a_cum = a_t * a_cum
        a_ref[tau * R:(tau + 1) * R, :] = a_cum
        b_ref[tau * R:(tau + 1) * R, :] = h_loc

    carry = h_ref[...]
    sh = 1
    while sh < R:
        keep = chunk >= sh
        a_prev = jnp.where(keep, pltpu.roll(a_cum, sh, 0), 1.0)
        h_prev = jnp.where(keep, pltpu.roll(h_loc, sh, 0), 0.0)
        h_loc = a_cum * h_prev + h_loc
        a_cum = a_cum * a_prev
        sh *= 2
    h_end = a_cum * carry + h_loc
    h_in = from_prev_chunk(h_end, carry)
    h_ref[...] = h_end[R - 1:R, :]

    c0 = math.sqrt(2.0 / math.pi)
    for tau in range(L):
        h = b_ref[tau * R:(tau + 1) * R, :] + a_ref[tau * R:(tau + 1) * R, :] * h_in
        g = chunk_rows(g_ref, tau)
        half_g = 0.5 * g
        gelu = half_g + half_g * jnp.tanh(g * (c0 + (c0 * 0.044715) * (g * g)))
        y = h * gelu
        for j in range(y_ref.shape[0]):
            y_ref[j, pl.ds(tau, R, stride=L), :] = y[:, j * LANES:(j + 1) * LANES]


def _rglru(x_lru, g_lru, conv_w, conv_b, wa_blk, ba, wx_blk, bx, lam, batch, seq):
    ts = LRU_CHUNKS * LRU_CHUNK_LEN
    nblk = seq // ts
    n_slab = LRU_WIDTH // LANES
    tile = pl.BlockSpec((n_slab, ts, LANES), lambda b, t: (0, b * nblk + t, 0))
    return pl.pallas_call(
        _rglru_kernel,
        grid=(batch, nblk),
        in_specs=[
            tile, tile,
            _const_spec(conv_w.shape), _const_spec(conv_b.shape),
            _const_spec(wa_blk.shape), _const_spec(ba.shape),
            _const_spec(wx_blk.shape), _const_spec(bx.shape), _const_spec(lam.shape),
        ],
        out_specs=tile,
        out_shape=jax.ShapeDtypeStruct((n_slab, batch * seq, LANES), F32),
        scratch_shapes=[
            pltpu.VMEM((ts, LRU_WIDTH), F32),
            pltpu.VMEM((ts, LRU_WIDTH), F32),
            pltpu.VMEM((CONV_WIDTH - 1, LRU_WIDTH), F32),
            pltpu.VMEM((1, LRU_WIDTH), F32),
        ],
        compiler_params=_params("parallel", "arbitrary"),
        name="rglru",
    )(x_lru, g_lru, conv_w, conv_b, wa_blk, ba, wx_blk, bx, lam)


def _merge_kernel(x_ref, ya_ref, yb_ref, ga_ref, gb_ref, wpa_ref, wpl_ref, wo_ref, o_ref):
    pa = _dot(ya_ref[...], wpa_ref[...])
    pb = _dot(_load_slabs(yb_ref).astype(BF16), wpl_ref[...])
    merged = _sigmoid(ga_ref[...].astype(F32)) * pa + _sigmoid(gb_ref[...].astype(F32)) * pb
    o_ref[...] = x_ref[...] + _dot(merged.astype(BF16), wo_ref[...])


def _merge(x2d, ya, yb, ga, gb, wpa, wpl, wo):
    m = x2d.shape[0]
    row = lambda i: (i, 0)
    tile = pl.BlockSpec((TM_PROJ, D_MODEL), row)
    slabs = pl.BlockSpec((yb.shape[0], TM_PROJ, LANES), lambda i: (0, i, 0))
    return pl.pallas_call(
        _merge_kernel,
        grid=(m // TM_PROJ,),
        in_specs=[tile, tile, slabs, tile, tile]
        + [_const_spec(wpa.shape), _const_spec(wpl.shape), _const_spec(wo.shape)],
        out_specs=tile,
        out_shape=jax.ShapeDtypeStruct((m, D_MODEL), F32),
        compiler_params=_params("parallel"),
        name="merge",
    )(x2d, ya, yb, ga, gb, wpa, wpl, wo)


def _ffn_kernel(x_ref, g_ref, wg_ref, wu_ref, wd_ref, o_ref):
    x = x_ref[...]
    h = _rms_rows(x, g_ref[...]).astype(BF16)
    gate = _dot(h, wg_ref[...])
    up = _dot(h, wu_ref[...])
    act = (gate * _sigmoid(gate) * up).astype(BF16)
    o_ref[...] = x + _dot(act, wd_ref[...])


def _ffn(x2d, norm_g, wg, wu, wd):
    m = x2d.shape[0]
    tm = TM_PROJ // 2
    row = lambda i: (i, 0)
    tile = pl.BlockSpec((tm, D_MODEL), row)
    return pl.pallas_call(
        _ffn_kernel,
        grid=(m // tm,),
        in_specs=[tile, _const_spec(norm_g.shape), _const_spec(wg.shape), _const_spec(wu.shape),
                  _const_spec(wd.shape)],
        out_specs=tile,
        out_shape=jax.ShapeDtypeStruct((m, D_MODEL), F32),
        compiler_params=_params("parallel"),
        name="ffn",
    )(x2d, norm_g, wg, wu, wd)


def _block_diag_groups(w):
    per = MXU_DIM // LRU_BLOCK
    n_grp = w.shape[0] // per
    w4 = w.reshape(n_grp, per, LRU_BLOCK, LRU_BLOCK)
    eye = jnp.eye(per, dtype=w.dtype)
    out = jnp.einsum("gpde,pq->gpdqe", w4, eye)
    return out.reshape(n_grp, MXU_DIM, MXU_DIM)


def kernel(x, positions, norm_mix_g, w_in, q_lat_g, w_q_up, kv_lat_g, w_kv_up, q_head_g, k_head_g,
           conv_w, conv_b, lru_wa, lru_ba, lru_wx, lru_bx, lru_lambda, w_proj_attn, w_proj_lru,
           w_out, norm_ffn_g, w_ffn_gate, w_ffn_up, w_ffn_down):
    batch, seq, d = x.shape
    depth = w_in.shape[0]
    half = HALF_ROPE
    inv_freq = (ROPE_THETA ** (-jnp.arange(half, dtype=F32) / half)).reshape(half, 1)
    positions3 = positions.reshape(batch, 1, seq)
    x2d = x.reshape(batch * seq, d)

    for l in range(depth):
        w = w_in[l].astype(BF16)
        c0 = Q_RANK
        c1 = c0 + KV_RANK
        c2 = c1 + ROPE
        c3 = c2 + LRU_WIDTH
        c4 = c3 + LRU_WIDTH
        c5 = c4 + D_MODEL
        w_kr = jnp.pad(w[:, c1:c2], ((0, 0), (0, LANES - ROPE)))
        w_parts = [w[:, :c0], w[:, c0:c1], w_kr, w[:, c2:c3], w[:, c3:c4], w[:, c4:c5], w[:, c5:]]
        cq, ckv, kr, x_lru, g_lru, gate_a, gate_b = _in_proj(x2d, norm_mix_g[l].reshape(1, d), w_parts)

        wkv = w_kv_up[l].astype(BF16).reshape(KV_RANK, HEADS, NOPE + V_DIM)
        wkn = wkv[:, :, :NOPE].reshape(KV_RANK, HEADS * NOPE)
        wv_t = wkv[:, :, NOPE:].reshape(KV_RANK, HEADS * V_DIM).T
        wq_t = w_q_up[l].astype(BF16).T
        amax = lambda v: jnp.max(jnp.abs(v))
        gq, gk = q_head_g[l], k_head_g[l]
        bound2 = (QK_DIM ** -0.5 * LOG2_E) * (NOPE * amax(gq[:NOPE]) * amax(gk[:NOPE])
                                              + ROPE * amax(gq[NOPE:]) * amax(gk[NOPE:]))
        shift_ok = bound2 <= SHIFT_LIMIT_LOG2
        offset = jnp.where(shift_ok, -bound2, 0.0).reshape(1, 1)
        q_t, kn, krope, v_t = _mla_prep(
            cq, ckv, kr, positions3, inv_freq, offset,
            q_lat_g[l].reshape(1, Q_RANK), kv_lat_g[l].reshape(1, KV_RANK), wq_t, wkn, wv_t,
            q_head_g[l][:NOPE].reshape(NOPE, 1), q_head_g[l][NOPE:].reshape(ROPE, 1),
            k_head_g[l][:NOPE].reshape(1, NOPE), k_head_g[l][NOPE:].reshape(ROPE, 1),
            batch, seq)
        y_a = lax.cond(
            shift_ok,
            functools.partial(_attention_shifted, batch=batch, seq=seq),
            functools.partial(_attention_online, batch=batch, seq=seq),
            q_t, kn, krope, v_t).reshape(batch * seq, HEADS * V_DIM)

        y_b = _rglru(
            x_lru, g_lru, conv_w[l], conv_b[l].reshape(1, LRU_WIDTH),
            _block_diag_groups(lru_wa[l] * -LOG2_E).astype(BF16), (lru_ba[l] * -LOG2_E).reshape(1, LRU_WIDTH),
            _block_diag_groups(lru_wx[l] * -LOG2_E).astype(BF16), (lru_bx[l] * -LOG2_E).reshape(1, LRU_WIDTH),
            lru_lambda[l].reshape(1, LRU_WIDTH), batch, seq)

        x2d = _merge(x2d, y_a, y_b, gate_a, gate_b, w_proj_attn[l].astype(BF16),
                     w_proj_lru[l].astype(BF16), w_out[l].astype(BF16))
        x2d = _ffn(x2d, norm_ffn_g[l].reshape(1, d), w_ffn_gate[l].astype(BF16),
                   w_ffn_up[l].astype(BF16), w_ffn_down[l].astype(BF16))
    return x2d.reshape(batch, seq, d)
```

```python
import functools
import math

import jax
import jax.numpy as jnp
from jax import lax
from jax.experimental import pallas as pl
from jax.experimental.pallas import tpu as pltpu

D_MODEL = 1024
HEADS = 8
NOPE = 128
ROPE = 64
HALF_ROPE = ROPE // 2
QK_DIM = NOPE + ROPE
V_DIM = 128
Q_RANK = 256
KV_RANK = 256
ROPE_THETA = 10000.0
LRU_WIDTH = 1024
LRU_BLOCK = 64
CONV_WIDTH = 4
LRU_C = 8.0
EPS = 1e-6

LANES = 128
SUBLANES = 8
MXU_DIM = 256
QK_PAD = 2 * LANES
SUM_ROWS = 16

LOG2_E = math.log2(math.e)
SHIFT_LIMIT_LOG2 = 60.0

VMEM_LIMIT = 56 * 1024 * 1024

TM_PROJ = 512
TM_PREP = 512
TQ = 512
TK = 512
LRU_CHUNKS = 16
LRU_CHUNK_LEN = 16
SQRT_FLOOR = 1e-30
PAIRS_PER_TRIP = 34

F32 = jnp.float32
BF16 = jnp.bfloat16


def _params(*semantics):
    return pltpu.CompilerParams(dimension_semantics=semantics, vmem_limit_bytes=VMEM_LIMIT)


def _const_spec(shape):
    zeros = (0,) * len(shape)
    return pl.BlockSpec(shape, lambda *_: zeros, pipeline_mode=pl.Buffered(1))


def _sigmoid(v):
    return 1.0 / (1.0 + jnp.exp(-v))


def _rms_rows(v, gain_row):
    ms = jnp.mean(v * v, axis=-1, keepdims=True)
    return v * lax.rsqrt(ms + EPS) * gain_row


def _rms_cols(v, gain_col):
    ms = jnp.mean(v * v, axis=0, keepdims=True)
    return v * lax.rsqrt(ms + EPS) * gain_col


def _dot(a, b):
    return jnp.dot(a, b, preferred_element_type=F32)


def _dot_nt(a, b):
    return lax.dot_general(a, b, (((1,), (1,)), ((), ())), preferred_element_type=F32)


def _in_proj_kernel(x_ref, g_ref, wq_ref, wkv_ref, wkr_ref, wx_ref, wg_ref, wa_ref, wb_ref,
                    cq_ref, ckv_ref, kr_ref, xl_ref, gl_ref, ga_ref, gb_ref):
    h = _rms_rows(x_ref[...], g_ref[...]).astype(BF16)
    cq_ref[...] = _dot(h, wq_ref[...])
    ckv_ref[...] = _dot(h, wkv_ref[...])
    kr_ref[...] = _dot(h, wkr_ref[...])
    _store_slabs(xl_ref, _dot(h, wx_ref[...]))
    _store_slabs(gl_ref, _dot(h, wg_ref[...]))
    ga_ref[...] = _dot(h, wa_ref[...]).astype(BF16)
    gb_ref[...] = _dot(h, wb_ref[...]).astype(BF16)


def _store_slabs(ref, v):
    for j in range(ref.shape[0]):
        ref[j] = v[:, j * LANES:(j + 1) * LANES].astype(ref.dtype)


def _load_slabs(ref):
    return jnp.concatenate([ref[j] for j in range(ref.shape[0])], axis=1)


def _in_proj(x2d, norm_g, w_parts):
    m = x2d.shape[0]
    row = lambda i: (i, 0)
    slab = lambda i: (0, i, 0)
    n_slab = LRU_WIDTH // LANES
    plain = lambda n, dt: (pl.BlockSpec((TM_PROJ, n), row), jax.ShapeDtypeStruct((m, n), dt))
    slabs = (pl.BlockSpec((n_slab, TM_PROJ, LANES), slab), jax.ShapeDtypeStruct((n_slab, m, LANES), F32))
    outs = [plain(Q_RANK, F32), plain(KV_RANK, F32), plain(LANES, F32), slabs, slabs,
            plain(D_MODEL, BF16), plain(D_MODEL, BF16)]
    return pl.pallas_call(
        _in_proj_kernel,
        grid=(m // TM_PROJ,),
        in_specs=[pl.BlockSpec((TM_PROJ, D_MODEL), row), _const_spec((1, D_MODEL))]
        + [_const_spec(w.shape) for w in w_parts],
        out_specs=[o[0] for o in outs],
        out_shape=[o[1] for o in outs],
        compiler_params=_params("parallel"),
        name="in_proj",
    )(x2d, norm_g, *w_parts)


def _mla_prep_kernel(cq_ref, ckv_ref, kr_ref, pos_ref, freq_ref, off_ref, qlg_ref, kvlg_ref,
                     wqt_ref, wkn_ref, wvt_ref, gqn_ref, gqr_ref, gkn_ref, gkr_ref,
                     qt_ref, kn_ref, krope_ref, vt_ref):
    scale = QK_DIM ** -0.5 * LOG2_E
    tm = cq_ref.shape[0]
    pad_row = lax.broadcasted_iota(jnp.int32, (QK_PAD - QK_DIM, tm), 0)
    ang = freq_ref[...] * pos_ref[0].astype(F32)
    cos = jnp.cos(ang)
    sin = jnp.sin(ang)

    def rope_cols(v):
        x1, x2 = v[:HALF_ROPE], v[HALF_ROPE:]
        return x1 * cos - x2 * sin, x2 * cos + x1 * sin

    cqn = _rms_rows(cq_ref[...], qlg_ref[...]).astype(BF16)
    q_t = _dot_nt(wqt_ref[...], cqn)
    pad = jnp.where(pad_row == 0, off_ref[...], 0.0).astype(BF16)
    for h in range(HEADS):
        base = h * QK_DIM
        qn = _rms_cols(q_t[base:base + NOPE], gqn_ref[...]) * scale
        qr = _rms_cols(q_t[base + NOPE:base + QK_DIM], gqr_ref[...]) * scale
        o1, o2 = rope_cols(qr)
        q_pad = jnp.concatenate([qn.astype(BF16), o1.astype(BF16), o2.astype(BF16), pad], axis=0)
        for c in range(qt_ref.shape[2]):
            qt_ref[0, h, c] = q_pad[:, c * TQ:(c + 1) * TQ]

    ckvn = _rms_rows(ckv_ref[...], kvlg_ref[...]).astype(BF16)
    kn_all = _dot(ckvn, wkn_ref[...])
    v_t = _dot_nt(wvt_ref[...], ckvn)
    n_chunks = vt_ref.shape[2]
    for h in range(HEADS):
        kn = _rms_rows(kn_all[:, h * NOPE:(h + 1) * NOPE], gkn_ref[...])
        kn_ref[0, h] = kn.astype(BF16)
        for c in range(n_chunks):
            vt_ref[0, h, c] = v_t[h * V_DIM:(h + 1) * V_DIM, c * TK:(c + 1) * TK].astype(BF16)

    kr_t = kr_ref[...].T
    krn = _rms_cols(kr_t[:ROPE], gkr_ref[...])
    o1, o2 = rope_cols(krn)
    kr_out = jnp.concatenate([o1, o2, jnp.where(pad_row == 0, 1.0, 0.0)], axis=0)
    krope_ref[0] = kr_out.T.astype(BF16)


def _mla_prep(cq, ckv, kr, positions3, inv_freq, offset, q_lat_g, kv_lat_g, wq_t, wkn, wv_t,
              gqn, gqr, gkn, gkr, batch, seq):
    tm = TM_PREP
    nblk = seq // tm
    tok = lambda b, i: (b * nblk + i, 0)
    in_specs = [
        pl.BlockSpec((tm, Q_RANK), tok),
        pl.BlockSpec((tm, KV_RANK), tok),
        pl.BlockSpec((tm, LANES), tok),
        pl.BlockSpec((1, 1, tm), lambda b, i: (b, 0, i)),
        _const_spec(inv_freq.shape), _const_spec(offset.shape),
        _const_spec(q_lat_g.shape), _const_spec(kv_lat_g.shape),
        _const_spec(wq_t.shape), _const_spec(wkn.shape), _const_spec(wv_t.shape),
        _const_spec(gqn.shape), _const_spec(gqr.shape), _const_spec(gkn.shape), _const_spec(gkr.shape),
    ]
    out_specs = [
        pl.BlockSpec((1, HEADS, tm // TQ, QK_PAD, TQ), lambda b, i: (b, 0, i, 0, 0)),
        pl.BlockSpec((1, HEADS, tm, NOPE), lambda b, i: (b, 0, i, 0)),
        pl.BlockSpec((1, tm, LANES), lambda b, i: (b, i, 0)),
        pl.BlockSpec((1, HEADS, tm // TK, V_DIM, TK), lambda b, i: (b, 0, i, 0, 0)),
    ]
    out_shape = [
        jax.ShapeDtypeStruct((batch, HEADS, seq // TQ, QK_PAD, TQ), BF16),
        jax.ShapeDtypeStruct((batch, HEADS, seq, NOPE), BF16),
        jax.ShapeDtypeStruct((batch, seq, LANES), BF16),
        jax.ShapeDtypeStruct((batch, HEADS, seq // TK, V_DIM, TK), BF16),
    ]
    return pl.pallas_call(
        _mla_prep_kernel,
        grid=(batch, nblk),
        in_specs=in_specs,
        out_specs=out_specs,
        out_shape=out_shape,
        compiler_params=_params("parallel", "parallel"),
        name="mla_prep",
    )(cq, ckv, kr, positions3, inv_freq, offset, q_lat_g, kv_lat_g, wq_t, wkn, wv_t, gqn, gqr, gkn, gkr)


def _scores_t(qt_ref, kn_ref, kr_ref, i, j):
    start = pl.multiple_of(j * TK, TK)
    k = jnp.concatenate([kn_ref[0, 0, pl.ds(start, TK), :], kr_ref[0, pl.ds(start, TK), :]], axis=1)
    return _dot(k, qt_ref[0, 0, i])


def _causal_mask(s, i, j):
    key = j * TK + lax.broadcasted_iota(jnp.int32, s.shape, 0)
    qry = i * TQ + lax.broadcasted_iota(jnp.int32, s.shape, 1)
    return jnp.where(key <= qry, s, -jnp.inf)


def _attention_online_kernel(qt_ref, kn_ref, kr_ref, vt_ref, o_ref, m_ref, l_ref, acc_ref):
    i = pl.program_id(2)
    m_ref[...] = jnp.full(m_ref.shape, -jnp.inf, F32)
    l_ref[...] = jnp.zeros(l_ref.shape, F32)
    acc_ref[...] = jnp.zeros(acc_ref.shape, F32)

    def step(j, masked):
        s = _scores_t(qt_ref, kn_ref, kr_ref, 0, j)
        if masked:
            s = _causal_mask(s, i, j)
        m_old = m_ref[...]
        m_new = jnp.maximum(m_old, jnp.max(s, axis=0, keepdims=True))
        alpha = jnp.exp2(m_old - m_new)
        p = jnp.exp2(s - m_new)
        l_ref[...] = alpha * l_ref[...] + jnp.sum(p, axis=0, keepdims=True)
        acc_ref[...] = alpha * acc_ref[...] + _dot(vt_ref[0, 0, j], p.astype(BF16))
        m_ref[...] = m_new

    def body(j, carry):
        step(j, masked=False)
        return carry

    lax.fori_loop(0, i, body, 0)
    step(i, masked=True)
    out_t = acc_ref[...] / l_ref[...]
    o_ref[0] = out_t.T.astype(o_ref.dtype)


def _attention_shifted_kernel(itab_ref, jtab_ref, qt_ref, kn_ref, kr_ref, vt_ref, o_ref,
                              s0_ref, s1_ref, p0_ref, p1_ref, acc_ref, *, n_pairs, diag_per_trip):
    ones = jnp.ones((SUM_ROWS, TK), BF16)
    s_slots = (s0_ref, s1_ref)
    p_slots = (p0_ref, p1_ref)

    def scores(t, s_ref):
        s_ref[...] = _scores_t(qt_ref, kn_ref, kr_ref, itab_ref[t], jtab_ref[t])

    def probs(t, s_ref, p_ref, diagonal):
        s = s_ref[...]
        if diagonal:
            s = _causal_mask(s, itab_ref[t], jtab_ref[t])
        p_ref[...] = jnp.exp2(s).astype(BF16)

    def values(t, p_ref):
        v_aug = jnp.concatenate([vt_ref[0, 0, jtab_ref[t]], ones], axis=0)
        acc_ref[itab_ref[t]] += _dot(v_aug, p_ref[...])

    acc_ref[...] = jnp.zeros(acc_ref.shape, F32)
    scores(0, s0_ref)
    scores(1, s1_ref)
    probs(0, s0_ref, p0_ref, diagonal=True)

    def body(u, carry):
        for sub in range(PAIRS_PER_TRIP):
            t = PAIRS_PER_TRIP * u + sub
            slot = sub % 2
            scores(t + 2, s_slots[slot])
            probs(t + 1, s_slots[1 - slot], p_slots[1 - slot],
                  diagonal=(sub + 1) % PAIRS_PER_TRIP < diag_per_trip)
            values(t, p_slots[slot])
        return carry

    lax.fori_loop(0, n_pairs // PAIRS_PER_TRIP, body, 0)

    for i in range(acc_ref.shape[0]):
        out_t = acc_ref[i, 0:V_DIM, :] / acc_ref[i, V_DIM:V_DIM + 1, :]
        o_ref[0, i * TQ:(i + 1) * TQ, :] = out_t.T.astype(o_ref.dtype)


def _attention_shifted(q_t, kn, krope, v_t, batch, seq):
    nq = seq // TQ
    diag = [(i, i) for i in range(nq)]
    below = [(i, j) for i in range(nq) for j in range(i)]
    n_pairs = len(diag) + len(below)
    trips = n_pairs // PAIRS_PER_TRIP
    assert TQ == TK and n_pairs % PAIRS_PER_TRIP == 0 and PAIRS_PER_TRIP % 2 == 0 and nq % trips == 0
    diag_per_trip = nq // trips
    below_per_trip = len(below) // trips
    pairs = []
    for u in range(trips):
        pairs += diag[u * diag_per_trip:(u + 1) * diag_per_trip]
        pairs += below[u * below_per_trip:(u + 1) * below_per_trip]
    pairs = pairs + [pairs[-1]] * 2
    itab = jnp.asarray([p[0] for p in pairs], jnp.int32)
    jtab = jnp.asarray([p[1] for p in pairs], jnp.int32)
    grid_spec = pltpu.PrefetchScalarGridSpec(
        num_scalar_prefetch=2,
        grid=(batch, HEADS),
        in_specs=[
            pl.BlockSpec((1, 1, nq, QK_PAD, TQ), lambda b, h, *_: (b, h, 0, 0, 0)),
            pl.BlockSpec((1, 1, seq, NOPE), lambda b, h, *_: (b, h, 0, 0)),
            pl.BlockSpec((1, seq, LANES), lambda b, h, *_: (b, 0, 0)),
            pl.BlockSpec((1, 1, seq // TK, V_DIM, TK), lambda b, h, *_: (b, h, 0, 0, 0)),
        ],
        out_specs=pl.BlockSpec((1, seq, V_DIM), lambda b, h, *_: (b, 0, h)),
        scratch_shapes=[
            pltpu.VMEM((TK, TQ), F32), pltpu.VMEM((TK, TQ), F32),
            pltpu.VMEM((TK, TQ), BF16), pltpu.VMEM((TK, TQ), BF16),
            pltpu.VMEM((nq, V_DIM + SUM_ROWS, TQ), F32),
        ],
    )
    return pl.pallas_call(
        functools.partial(_attention_shifted_kernel, n_pairs=n_pairs, diag_per_trip=diag_per_trip),
        grid_spec=grid_spec,
        out_shape=jax.ShapeDtypeStruct((batch, seq, HEADS * V_DIM), BF16),
        compiler_params=_params("parallel", "parallel"),
        name="attention_shifted",
    )(itab, jtab, q_t, kn, krope, v_t)


def _attention_online(q_t, kn, krope, v_t, batch, seq):
    assert TQ == TK
    nq = seq // TQ
    return pl.pallas_call(
        _attention_online_kernel,
        grid=(batch, HEADS, nq),
        in_specs=[
            pl.BlockSpec((1, 1, 1, QK_PAD, TQ), lambda b, h, i: (b, h, i, 0, 0)),
            pl.BlockSpec((1, 1, seq, NOPE), lambda b, h, i: (b, h, 0, 0)),
            pl.BlockSpec((1, seq, LANES), lambda b, h, i: (b, 0, 0)),
            pl.BlockSpec((1, 1, seq // TK, V_DIM, TK), lambda b, h, i: (b, h, 0, 0, 0)),
        ],
        out_specs=pl.BlockSpec((1, TQ, V_DIM), lambda b, h, i: (b, i, h)),
        out_shape=jax.ShapeDtypeStruct((batch, seq, HEADS * V_DIM), BF16),
        scratch_shapes=[pltpu.VMEM((1, TQ), F32), pltpu.VMEM((1, TQ), F32), pltpu.VMEM((V_DIM, TQ), F32)],
        compiler_params=_params("parallel", "parallel", "arbitrary"),
        name="attention_online",
    )(q_t, kn, krope, v_t)


def _rglru_kernel(x_ref, g_ref, cw_ref, cb_ref, wa_ref, ba_ref, wx_ref, bx_ref, lam_ref,
                  y_ref, a_ref, b_ref, tail_ref, h_ref):
    t = pl.program_id(1)

    @pl.when(t == 0)
    def _():
        tail_ref[...] = jnp.zeros(tail_ref.shape, F32)
        h_ref[...] = jnp.zeros(h_ref.shape, F32)

    def group(n, carry):
        _rglru_group(n, x_ref, g_ref, cw_ref, cb_ref, wa_ref, ba_ref, wx_ref, bx_ref, lam_ref,
                     y_ref, a_ref, b_ref, tail_ref, h_ref)
        return carry

    lax.fori_loop(0, LRU_WIDTH // MXU_DIM, group, 0, unroll=2)


def _rglru_group(n, x_ref, g_ref, cw_ref, cb_ref, wa_ref, ba_ref, wx_ref, bx_ref, lam_ref,
                 y_ref, a_ref, b_ref, tail_ref, h_ref):
    R, L, W = LRU_CHUNKS, LRU_CHUNK_LEN, MXU_DIM
    slabs = W // LANES
    chunk = lax.broadcasted_iota(jnp.int32, (R, W), 0)

    def chunk_rows(ref, tau):
        return jnp.concatenate(
            [ref[n * slabs + j, pl.ds(tau, R, stride=L), :] for j in range(slabs)], axis=1)

    def step_rows(v, tau):
        return v[tau * R:(tau + 1) * R]

    def from_prev_chunk(v, first):
        return jnp.where(chunk == 0, first, pltpu.roll(v, 1, 0))

    x = jnp.concatenate([chunk_rows(x_ref, tau) for tau in range(L)], axis=0)
    tails = tail_ref[n]
    lead = [from_prev_chunk(step_rows(x, L - m), tails[m - 1:m, :]) for m in range(CONV_WIDTH - 1, 0, -1)]
    ext = jnp.concatenate(lead + [x], axis=0)
    tail_ref[n] = jnp.concatenate(
        [x[(L - m + 1) * R - 1:(L - m + 1) * R] for m in range(1, CONV_WIDTH)], axis=0)

    cw = cw_ref[n]
    xc = cb_ref[n] + ext[0:L * R] * cw[0:1, :]
    for tap in range(1, CONV_WIDTH):
        xc = xc + ext[tap * R:(tap + L) * R] * cw[tap:tap + 1, :]

    xcb = xc.astype(BF16)
    r = 1.0 / (1.0 + jnp.exp2(_dot(xcb, wa_ref[n]) + ba_ref[n]))
    gi = 1.0 / (1.0 + jnp.exp2(_dot(xcb, wx_ref[n]) + bx_ref[n]))

    lam = lam_ref[n]
    softplus_neg = jnp.maximum(-lam, 0.0) + jnp.log1p(jnp.exp(-jnp.abs(lam)))
    neg_log_a = r * (LRU_C * softplus_neg)
    a = jnp.exp2(neg_log_a * (-LOG2_E))
    m2 = jnp.tanh(neg_log_a) * (1.0 + a * a)
    mult = m2 * lax.rsqrt(jnp.maximum(m2, SQRT_FLOOR))
    a_ref[...] = a
    b_ref[...] = mult * (gi * xc)

    a_cum = a_ref[0:R, :]
    h_loc = b_ref[0:R, :]
    for tau in range(1, L):
        a_t = a_ref[tau * R:(tau + 1) * R, :]
        h_loc = a_t * h_loc + b_ref[tau * R:(tau + 1) * R, :]
        a_cum = a_t * a_cum
        a_ref[tau * R:(tau + 1) * R, :] = a_cum
        b_ref[tau * R:(tau + 1) * R, :] = h_loc

    carry = h_ref[n]
    sh = 1
    while sh < R:
        keep = chunk >= sh
        a_prev = jnp.where(keep, pltpu.roll(a_cum, sh, 0), 1.0)
        h_prev = jnp.where(keep, pltpu.roll(h_loc, sh, 0), 0.0)
        h_loc = a_cum * h_prev + h_loc
        a_cum = a_cum * a_prev
        sh *= 2
    h_end = a_cum * carry + h_loc
    h_in = from_prev_chunk(h_end, carry)
    h_ref[n] = h_end[R - 1:R, :]

    c0 = math.sqrt(2.0 / math.pi)
    for tau in range(L):
        h = b_ref[tau * R:(tau + 1) * R, :] + a_ref[tau * R:(tau + 1) * R, :] * h_in
        g = chunk_rows(g_ref, tau)
        half_g = 0.5 * g
        gelu = half_g + half_g * jnp.tanh(g * (c0 + (c0 * 0.044715) * (g * g)))
        y = h * gelu
        for j in range(slabs):
            y_ref[n * slabs + j, pl.ds(tau, R, stride=L), :] = y[:, j * LANES:(j + 1) * LANES]


def _rglru(x_lru, g_lru, conv_w, conv_b, wa_blk, ba, wx_blk, bx, lam, batch, seq):
    ts = LRU_CHUNKS * LRU_CHUNK_LEN
    nblk = seq // ts
    n_slab = LRU_WIDTH // LANES
    n_grp = LRU_WIDTH // MXU_DIM
    per_group = lambda v: v.reshape(-1, n_grp, MXU_DIM).transpose(1, 0, 2)
    conv_w, conv_b, ba, bx, lam = (per_group(v) for v in (conv_w, conv_b, ba, bx, lam))
    tile = pl.BlockSpec((n_slab, ts, LANES), lambda b, t: (0, b * nblk + t, 0))
    return pl.pallas_call(
        _rglru_kernel,
        grid=(batch, nblk),
        in_specs=[
            tile, tile,
            _const_spec(conv_w.shape), _const_spec(conv_b.shape),
            _const_spec(wa_blk.shape), _const_spec(ba.shape),
            _const_spec(wx_blk.shape), _const_spec(bx.shape), _const_spec(lam.shape),
        ],
        out_specs=tile,
        out_shape=jax.ShapeDtypeStruct((n_slab, batch * seq, LANES), F32),
        scratch_shapes=[
            pltpu.VMEM((ts, MXU_DIM), F32),
            pltpu.VMEM((ts, MXU_DIM), F32),
            pltpu.VMEM((n_grp, CONV_WIDTH - 1, MXU_DIM), F32),
            pltpu.VMEM((n_grp, 1, MXU_DIM), F32),
        ],
        compiler_params=_params("parallel", "arbitrary"),
        name="rglru",
    )(x_lru, g_lru, conv_w, conv_b, wa_blk, ba, wx_blk, bx, lam)


def _merge_kernel(x_ref, ya_ref, yb_ref, ga_ref, gb_ref, wpa_ref, wpl_ref, wo_ref, o_ref):
    pa = _dot(ya_ref[...], wpa_ref[...])
    pb = _dot(_load_slabs(yb_ref).astype(BF16), wpl_ref[...])
    merged = _sigmoid(ga_ref[...].astype(F32)) * pa + _sigmoid(gb_ref[...].astype(F32)) * pb
    o_ref[...] = x_ref[...] + _dot(merged.astype(BF16), wo_ref[...])


def _merge(x2d, ya, yb, ga, gb, wpa, wpl, wo):
    m = x2d.shape[0]
    row = lambda i: (i, 0)
    tile = pl.BlockSpec((TM_PROJ, D_MODEL), row)
    slabs = pl.BlockSpec((yb.shape[0], TM_PROJ, LANES), lambda i: (0, i, 0))
    return pl.pallas_call(
        _merge_kernel,
        grid=(m // TM_PROJ,),
        in_specs=[tile, tile, slabs, tile, tile]
        + [_const_spec(wpa.shape), _const_spec(wpl.shape), _const_spec(wo.shape)],
        out_specs=tile,
        out_shape=jax.ShapeDtypeStruct((m, D_MODEL), F32),
        compiler_params=_params("parallel"),
        name="merge",
    )(x2d, ya, yb, ga, gb, wpa, wpl, wo)


def _ffn_kernel(x_ref, g_ref, wg_ref, wu_ref, wd_ref, o_ref):
    x = x_ref[...]
    h = _rms_rows(x, g_ref[...]).astype(BF16)
    gate = _dot(h, wg_ref[...])
    up = _dot(h, wu_ref[...])
    act = (gate * _sigmoid(gate) * up).astype(BF16)
    o_ref[...] = x + _dot(act, wd_ref[...])


def _ffn(x2d, norm_g, wg, wu, wd):
    m = x2d.shape[0]
    tm = TM_PROJ
    row = lambda i: (i, 0)
    tile = pl.BlockSpec((tm, D_MODEL), row)
    return pl.pallas_call(
        _ffn_kernel,
        grid=(m // tm,),
        in_specs=[tile, _const_spec(norm_g.shape), _const_spec(wg.shape), _const_spec(wu.shape),
                  _const_spec(wd.shape)],
        out_specs=tile,
        out_shape=jax.ShapeDtypeStruct((m, D_MODEL), F32),
        compiler_params=_params("parallel"),
        name="ffn",
    )(x2d, norm_g, wg, wu, wd)


def _block_diag_groups(w):
    per = MXU_DIM // LRU_BLOCK
    n_grp = w.shape[0] // per
    w4 = w.reshape(n_grp, per, LRU_BLOCK, LRU_BLOCK)
    eye = jnp.eye(per, dtype=w.dtype)
    out = jnp.einsum("gpde,pq->gpdqe", w4, eye)
    return out.reshape(n_grp, MXU_DIM, MXU_DIM)


def kernel(x, positions, norm_mix_g, w_in, q_lat_g, w_q_up, kv_lat_g, w_kv_up, q_head_g, k_head_g,
           conv_w, conv_b, lru_wa, lru_ba, lru_wx, lru_bx, lru_lambda, w_proj_attn, w_proj_lru,
           w_out, norm_ffn_g, w_ffn_gate, w_ffn_up, w_ffn_down):
    batch, seq, d = x.shape
    depth = w_in.shape[0]
    half = HALF_ROPE
    inv_freq = (ROPE_THETA ** (-jnp.arange(half, dtype=F32) / half)).reshape(half, 1)
    positions3 = positions.reshape(batch, 1, seq)
    x2d = x.reshape(batch * seq, d)

    for l in range(depth):
        w = w_in[l].astype(BF16)
        c0 = Q_RANK
        c1 = c0 + KV_RANK
        c2 = c1 + ROPE
        c3 = c2 + LRU_WIDTH
        c4 = c3 + LRU_WIDTH
        c5 = c4 + D_MODEL
        w_kr = jnp.pad(w[:, c1:c2], ((0, 0), (0, LANES - ROPE)))
        w_parts = [w[:, :c0], w[:, c0:c1], w_kr, w[:, c2:c3], w[:, c3:c4], w[:, c4:c5], w[:, c5:]]
        cq, ckv, kr, x_lru, g_lru, gate_a, gate_b = _in_proj(x2d, norm_mix_g[l].reshape(1, d), w_parts)

        wkv = w_kv_up[l].astype(BF16).reshape(KV_RANK, HEADS, NOPE + V_DIM)
        wkn = wkv[:, :, :NOPE].reshape(KV_RANK, HEADS * NOPE)
        wv_t = wkv[:, :, NOPE:].reshape(KV_RANK, HEADS * V_DIM).T
        wq_t = w_q_up[l].astype(BF16).T
        amax = lambda v: jnp.max(jnp.abs(v))
        gq, gk = q_head_g[l], k_head_g[l]
        bound2 = (QK_DIM ** -0.5 * LOG2_E) * (NOPE * amax(gq[:NOPE]) * amax(gk[:NOPE])
                                              + ROPE * amax(gq[NOPE:]) * amax(gk[NOPE:]))
        shift_ok = bound2 <= SHIFT_LIMIT_LOG2
        offset = jnp.where(shift_ok, -bound2, 0.0).reshape(1, 1)
        q_t, kn, krope, v_t = _mla_prep(
            cq, ckv, kr, positions3, inv_freq, offset,
            q_lat_g[l].reshape(1, Q_RANK), kv_lat_g[l].reshape(1, KV_RANK), wq_t, wkn, wv_t,
            q_head_g[l][:NOPE].reshape(NOPE, 1), q_head_g[l][NOPE:].reshape(ROPE, 1),
            k_head_g[l][:NOPE].reshape(1, NOPE), k_head_g[l][NOPE:].reshape(ROPE, 1),
            batch, seq)
        y_a = lax.cond(
            shift_ok,
            functools.partial(_attention_shifted, batch=batch, seq=seq),
            functools.partial(_attention_online, batch=batch, seq=seq),
            q_t, kn, krope, v_t).reshape(batch * seq, HEADS * V_DIM)

        y_b = _rglru(
            x_lru, g_lru, conv_w[l], conv_b[l].reshape(1, LRU_WIDTH),
            _block_diag_groups(lru_wa[l] * -LOG2_E).astype(BF16), (lru_ba[l] * -LOG2_E).reshape(1, LRU_WIDTH),
            _block_diag_groups(lru_wx[l] * -LOG2_E).astype(BF16), (lru_bx[l] * -LOG2_E).reshape(1, LRU_WIDTH),
            lru_lambda[l].reshape(1, LRU_WIDTH), batch, seq)

        x2d = _merge(x2d, y_a, y_b, gate_a, gate_b, w_proj_attn[l].astype(BF16),
                     w_proj_lru[l].astype(BF16), w_out[l].astype(BF16))
        x2d = _ffn(x2d, norm_ffn_g[l].reshape(1, d), w_ffn_gate[l].astype(BF16),
                   w_ffn_up[l].astype(BF16), w_ffn_down[l].astype(BF16))
    return x2d.reshape(batch, seq, d)
```

```python
import functools
import math

import jax
import jax.numpy as jnp
from jax import lax
from jax.experimental import pallas as pl
from jax.experimental.pallas import tpu as pltpu

D_MODEL = 1024
HEADS = 8
NOPE = 128
ROPE = 64
HALF_ROPE = ROPE // 2
QK_DIM = NOPE + ROPE
V_DIM = 128
Q_RANK = 256
KV_RANK = 256
ROPE_THETA = 10000.0
LRU_WIDTH = 1024
LRU_BLOCK = 64
CONV_WIDTH = 4
LRU_C = 8.0
EPS = 1e-6

LANES = 128
SUBLANES = 8
MXU_DIM = 256
QK_PAD = 2 * LANES
SUM_ROWS = 16

LOG2_E = math.log2(math.e)
SHIFT_LIMIT_LOG2 = 60.0

VMEM_LIMIT = 56 * 1024 * 1024

TM_PROJ = 1024
TM_FFN = 512
TM_PREP = 512
TQ = 512
TK = 512
LRU_CHUNKS = 16
LRU_CHUNK_LEN = 16
SQRT_FLOOR = 1e-30
PAIRS_PER_TRIP = 34

F32 = jnp.float32
BF16 = jnp.bfloat16


def _params(*semantics):
    return pltpu.CompilerParams(dimension_semantics=semantics, vmem_limit_bytes=VMEM_LIMIT)


def _const_spec(shape):
    zeros = (0,) * len(shape)
    return pl.BlockSpec(shape, lambda *_: zeros, pipeline_mode=pl.Buffered(1))


def _sigmoid(v):
    return 1.0 / (1.0 + jnp.exp(-v))


def _rms_rows(v, gain_row):
    ms = jnp.mean(v * v, axis=-1, keepdims=True)
    return v * lax.rsqrt(ms + EPS) * gain_row


def _rms_cols(v, gain_col):
    ms = jnp.mean(v * v, axis=0, keepdims=True)
    return v * lax.rsqrt(ms + EPS) * gain_col


def _dot(a, b):
    return jnp.dot(a, b, preferred_element_type=F32)


def _dot_nt(a, b):
    return lax.dot_general(a, b, (((1,), (1,)), ((), ())), preferred_element_type=F32)


def _in_proj_kernel(x_ref, g_ref, wq_ref, wkv_ref, wkr_ref, wx_ref, wg_ref, wa_ref, wb_ref,
                    cq_ref, ckv_ref, kr_ref, xl_ref, gl_ref, ga_ref, gb_ref):
    h = _rms_rows(x_ref[...], g_ref[...]).astype(BF16)
    cq_ref[...] = _dot(h, wq_ref[...])
    ckv_ref[...] = _dot(h, wkv_ref[...])
    kr_ref[...] = _dot(h, wkr_ref[...])
    _store_slabs(xl_ref, _dot(h, wx_ref[...]))
    _store_slabs(gl_ref, _dot(h, wg_ref[...]))
    ga_ref[...] = _dot(h, wa_ref[...]).astype(BF16)
    gb_ref[...] = _dot(h, wb_ref[...]).astype(BF16)


def _store_slabs(ref, v):
    for j in range(ref.shape[0]):
        ref[j] = v[:, j * LANES:(j + 1) * LANES].astype(ref.dtype)


def _load_slabs(ref):
    return jnp.concatenate([ref[j] for j in range(ref.shape[0])], axis=1)


def _in_proj(x2d, norm_g, w_parts):
    m = x2d.shape[0]
    row = lambda i: (i, 0)
    slab = lambda i: (0, i, 0)
    n_slab = LRU_WIDTH // LANES
    plain = lambda n, dt: (pl.BlockSpec((TM_PROJ, n), row), jax.ShapeDtypeStruct((m, n), dt))
    slabs = (pl.BlockSpec((n_slab, TM_PROJ, LANES), slab), jax.ShapeDtypeStruct((n_slab, m, LANES), F32))
    outs = [plain(Q_RANK, F32), plain(KV_RANK, F32), plain(LANES, F32), slabs, slabs,
            plain(D_MODEL, BF16), plain(D_MODEL, BF16)]
    return pl.pallas_call(
        _in_proj_kernel,
        grid=(m // TM_PROJ,),
        in_specs=[pl.BlockSpec((TM_PROJ, D_MODEL), row), _const_spec((1, D_MODEL))]
        + [_const_spec(w.shape) for w in w_parts],
        out_specs=[o[0] for o in outs],
        out_shape=[o[1] for o in outs],
        compiler_params=_params("parallel"),
        name="in_proj",
    )(x2d, norm_g, *w_parts)


def _mla_prep_kernel(cq_ref, ckv_ref, kr_ref, pos_ref, freq_ref, off_ref, qlg_ref, kvlg_ref,
                     wqt_ref, wkn_ref, wvt_ref, gqn_ref, gqr_ref, gkn_ref, gkr_ref,
                     qt_ref, kn_ref, krope_ref, vt_ref):
    scale = QK_DIM ** -0.5 * LOG2_E
    tm = cq_ref.shape[0]
    pad_row = lax.broadcasted_iota(jnp.int32, (QK_PAD - QK_DIM, tm), 0)
    ang = freq_ref[...] * pos_ref[0].astype(F32)
    cos = jnp.cos(ang)
    sin = jnp.sin(ang)

    def rope_cols(v):
        x1, x2 = v[:HALF_ROPE], v[HALF_ROPE:]
        return x1 * cos - x2 * sin, x2 * cos + x1 * sin

    cqn = _rms_rows(cq_ref[...], qlg_ref[...]).astype(BF16)
    q_t = _dot_nt(wqt_ref[...], cqn)
    pad = jnp.where(pad_row == 0, off_ref[...], 0.0).astype(BF16)
    for h in range(HEADS):
        base = h * QK_DIM
        qn = _rms_cols(q_t[base:base + NOPE], gqn_ref[...]) * scale
        qr = _rms_cols(q_t[base + NOPE:base + QK_DIM], gqr_ref[...]) * scale
        o1, o2 = rope_cols(qr)
        q_pad = jnp.concatenate([qn.astype(BF16), o1.astype(BF16), o2.astype(BF16), pad], axis=0)
        for c in range(qt_ref.shape[2]):
            qt_ref[0, h, c] = q_pad[:, c * TQ:(c + 1) * TQ]

    ckvn = _rms_rows(ckv_ref[...], kvlg_ref[...]).astype(BF16)
    kn_all = _dot(ckvn, wkn_ref[...])
    v_t = _dot_nt(wvt_ref[...], ckvn)
    n_chunks = vt_ref.shape[2]
    for h in range(HEADS):
        kn = _rms_rows(kn_all[:, h * NOPE:(h + 1) * NOPE], gkn_ref[...])
        kn_ref[0, h] = kn.astype(BF16)
        for c in range(n_chunks):
            vt_ref[0, h, c] = v_t[h * V_DIM:(h + 1) * V_DIM, c * TK:(c + 1) * TK].astype(BF16)

    kr_t = kr_ref[...].T
    krn = _rms_cols(kr_t[:ROPE], gkr_ref[...])
    o1, o2 = rope_cols(krn)
    kr_out = jnp.concatenate([o1, o2, jnp.where(pad_row == 0, 1.0, 0.0)], axis=0)
    krope_ref[0] = kr_out.T.astype(BF16)


def _mla_prep(cq, ckv, kr, positions3, inv_freq, offset, q_lat_g, kv_lat_g, wq_t, wkn, wv_t,
              gqn, gqr, gkn, gkr, batch, seq):
    tm = TM_PREP
    nblk = seq // tm
    tok = lambda b, i: (b * nblk + i, 0)
    in_specs = [
        pl.BlockSpec((tm, Q_RANK), tok),
        pl.BlockSpec((tm, KV_RANK), tok),
        pl.BlockSpec((tm, LANES), tok),
        pl.BlockSpec((1, 1, tm), lambda b, i: (b, 0, i)),
        _const_spec(inv_freq.shape), _const_spec(offset.shape),
        _const_spec(q_lat_g.shape), _const_spec(kv_lat_g.shape),
        _const_spec(wq_t.shape), _const_spec(wkn.shape), _const_spec(wv_t.shape),
        _const_spec(gqn.shape), _const_spec(gqr.shape), _const_spec(gkn.shape), _const_spec(gkr.shape),
    ]
    out_specs = [
        pl.BlockSpec((1, HEADS, tm // TQ, QK_PAD, TQ), lambda b, i: (b, 0, i, 0, 0)),
        pl.BlockSpec((1, HEADS, tm, NOPE), lambda b, i: (b, 0, i, 0)),
        pl.BlockSpec((1, tm, LANES), lambda b, i: (b, i, 0)),
        pl.BlockSpec((1, HEADS, tm // TK, V_DIM, TK), lambda b, i: (b, 0, i, 0, 0)),
    ]
    out_shape = [
        jax.ShapeDtypeStruct((batch, HEADS, seq // TQ, QK_PAD, TQ), BF16),
        jax.ShapeDtypeStruct((batch, HEADS, seq, NOPE), BF16),
        jax.ShapeDtypeStruct((batch, seq, LANES), BF16),
        jax.ShapeDtypeStruct((batch, HEADS, seq // TK, V_DIM, TK), BF16),
    ]
    return pl.pallas_call(
        _mla_prep_kernel,
        grid=(batch, nblk),
        in_specs=in_specs,
        out_specs=out_specs,
        out_shape=out_shape,
        compiler_params=_params("parallel", "parallel"),
        name="mla_prep",
    )(cq, ckv, kr, positions3, inv_freq, offset, q_lat_g, kv_lat_g, wq_t, wkn, wv_t, gqn, gqr, gkn, gkr)


def _scores_t(qt_ref, kn_ref, kr_ref, i, j):
    start = pl.multiple_of(j * TK, TK)
    k = jnp.concatenate([kn_ref[0, 0, pl.ds(start, TK), :], kr_ref[0, pl.ds(start, TK), :]], axis=1)
    return _dot(k, qt_ref[0, 0, i])


def _causal_mask(s, i, j):
    key = j * TK + lax.broadcasted_iota(jnp.int32, s.shape, 0)
    qry = i * TQ + lax.broadcasted_iota(jnp.int32, s.shape, 1)
    return jnp.where(key <= qry, s, -jnp.inf)


def _attention_online_kernel(qt_ref, kn_ref, kr_ref, vt_ref, o_ref, m_ref, l_ref, acc_ref):
    i = pl.program_id(2)
    m_ref[...] = jnp.full(m_ref.shape, -jnp.inf, F32)
    l_ref[...] = jnp.zeros(l_ref.shape, F32)
    acc_ref[...] = jnp.zeros(acc_ref.shape, F32)

    def step(j, masked):
        s = _scores_t(qt_ref, kn_ref, kr_ref, 0, j)
        if masked:
            s = _causal_mask(s, i, j)
        m_old = m_ref[...]
        m_new = jnp.maximum(m_old, jnp.max(s, axis=0, keepdims=True))
        alpha = jnp.exp2(m_old - m_new)
        p = jnp.exp2(s - m_new)
        l_ref[...] = alpha * l_ref[...] + jnp.sum(p, axis=0, keepdims=True)
        acc_ref[...] = alpha * acc_ref[...] + _dot(vt_ref[0, 0, j], p.astype(BF16))
        m_ref[...] = m_new

    def body(j, carry):
        step(j, masked=False)
        return carry

    lax.fori_loop(0, i, body, 0)
    step(i, masked=True)
    out_t = acc_ref[...] / l_ref[...]
    o_ref[0] = out_t.T.astype(o_ref.dtype)


def _attention_shifted_kernel(itab_ref, jtab_ref, qt_ref, kn_ref, kr_ref, vt_ref, o_ref,
                              s0_ref, s1_ref, p0_ref, p1_ref, acc_ref, *, n_pairs, diag_per_trip):
    ones = jnp.ones((SUM_ROWS, TK), BF16)
    s_slots = (s0_ref, s1_ref)
    p_slots = (p0_ref, p1_ref)
    half = TK // 2
    assert TQ == TK and half % LANES == 0 and diag_per_trip >= 2

    def scores(t, s_ref, diagonal):
        if not diagonal:
            s_ref[...] = _scores_t(qt_ref, kn_ref, kr_ref, itab_ref[t], jtab_ref[t])
            return
        start = pl.multiple_of(jtab_ref[t] * TK, TK)
        k = jnp.concatenate([kn_ref[0, 0, pl.ds(start, TK), :], kr_ref[0, pl.ds(start, TK), :]], axis=1)
        q_t = qt_ref[0, 0, itab_ref[t]]
        s_ref[0:half, :] = _dot(k[0:half], q_t)
        s_ref[half:, half:] = _dot(k[half:], q_t[:, half:])

    def probs(t, s_ref, p_ref, diagonal):
        if not diagonal:
            p_ref[...] = jnp.exp2(s_ref[...]).astype(BF16)
            return

        def masked_exp2(s):
            key = lax.broadcasted_iota(jnp.int32, s.shape, 0)
            qry = lax.broadcasted_iota(jnp.int32, s.shape, 1)
            return jnp.exp2(jnp.where(key <= qry, s, -jnp.inf)).astype(BF16)

        p_ref[0:half, :] = masked_exp2(s_ref[0:half, :])
        p_ref[half:, half:] = masked_exp2(s_ref[half:, half:])

    def values(t, p_ref, diagonal):
        i = itab_ref[t]
        v_aug = jnp.concatenate([vt_ref[0, 0, jtab_ref[t]], ones], axis=0)
        if not diagonal:
            acc_ref[i] += _dot(v_aug, p_ref[...])
            return
        acc_ref[i, :, 0:half] += _dot(v_aug[:, 0:half], p_ref[0:half, 0:half])
        acc_ref[i, :, half:] += _dot(v_aug, p_ref[:, half:])

    def is_diagonal(position):
        return position % PAIRS_PER_TRIP < diag_per_trip

    acc_ref[...] = jnp.zeros(acc_ref.shape, F32)
    scores(0, s0_ref, diagonal=True)
    scores(1, s1_ref, diagonal=True)
    probs(0, s0_ref, p0_ref, diagonal=True)

    def body(u, carry):
        for sub in range(PAIRS_PER_TRIP):
            t = PAIRS_PER_TRIP * u + sub
            slot = sub % 2
            scores(t + 2, s_slots[slot], is_diagonal(sub + 2))
            probs(t + 1, s_slots[1 - slot], p_slots[1 - slot], is_diagonal(sub + 1))
            values(t, p_slots[slot], is_diagonal(sub))
        return carry

    lax.fori_loop(0, n_pairs // PAIRS_PER_TRIP, body, 0)

    for i in range(acc_ref.shape[0]):
        out_t = acc_ref[i, 0:V_DIM, :] / acc_ref[i, V_DIM:V_DIM + 1, :]
        o_ref[0, i * TQ:(i + 1) * TQ, :] = out_t.T.astype(o_ref.dtype)


def _attention_shifted(q_t, kn, krope, v_t, batch, seq):
    nq = seq // TQ
    diag = [(i, i) for i in range(nq)]
    below = [(i, j) for i in range(nq) for j in range(i)]
    n_pairs = len(diag) + len(below)
    trips = n_pairs // PAIRS_PER_TRIP
    assert TQ == TK and n_pairs % PAIRS_PER_TRIP == 0 and PAIRS_PER_TRIP % 2 == 0 and nq % trips == 0
    diag_per_trip = nq // trips
    below_per_trip = len(below) // trips
    pairs = []
    for u in range(trips):
        pairs += diag[u * diag_per_trip:(u + 1) * diag_per_trip]
        pairs += below[u * below_per_trip:(u + 1) * below_per_trip]
    pairs = pairs + [pairs[-1]] * 2
    itab = jnp.asarray([p[0] for p in pairs], jnp.int32)
    jtab = jnp.asarray([p[1] for p in pairs], jnp.int32)
    grid_spec = pltpu.PrefetchScalarGridSpec(
        num_scalar_prefetch=2,
        grid=(batch, HEADS),
        in_specs=[
            pl.BlockSpec((1, 1, nq, QK_PAD, TQ), lambda b, h, *_: (b, h, 0, 0, 0)),
            pl.BlockSpec((1, 1, seq, NOPE), lambda b, h, *_: (b, h, 0, 0)),
            pl.BlockSpec((1, seq, LANES), lambda b, h, *_: (b, 0, 0)),
            pl.BlockSpec((1, 1, seq // TK, V_DIM, TK), lambda b, h, *_: (b, h, 0, 0, 0)),
        ],
        out_specs=pl.BlockSpec((1, seq, V_DIM), lambda b, h, *_: (b, 0, h)),
        scratch_shapes=[
            pltpu.VMEM((TK, TQ), F32), pltpu.VMEM((TK, TQ), F32),
            pltpu.VMEM((TK, TQ), BF16), pltpu.VMEM((TK, TQ), BF16),
            pltpu.VMEM((nq, V_DIM + SUM_ROWS, TQ), F32),
        ],
    )
    return pl.pallas_call(
        functools.partial(_attention_shifted_kernel, n_pairs=n_pairs, diag_per_trip=diag_per_trip),
        grid_spec=grid_spec,
        out_shape=jax.ShapeDtypeStruct((batch, seq, HEADS * V_DIM), BF16),
        compiler_params=_params("parallel", "parallel"),
        name="attention_shifted",
    )(itab, jtab, q_t, kn, krope, v_t)


def _attention_online(q_t, kn, krope, v_t, batch, seq):
    assert TQ == TK
    nq = seq // TQ
    return pl.pallas_call(
        _attention_online_kernel,
        grid=(batch, HEADS, nq),
        in_specs=[
            pl.BlockSpec((1, 1, 1, QK_PAD, TQ), lambda b, h, i: (b, h, i, 0, 0)),
            pl.BlockSpec((1, 1, seq, NOPE), lambda b, h, i: (b, h, 0, 0)),
            pl.BlockSpec((1, seq, LANES), lambda b, h, i: (b, 0, 0)),
            pl.BlockSpec((1, 1, seq // TK, V_DIM, TK), lambda b, h, i: (b, h, 0, 0, 0)),
        ],
        out_specs=pl.BlockSpec((1, TQ, V_DIM), lambda b, h, i: (b, i, h)),
        out_shape=jax.ShapeDtypeStruct((batch, seq, HEADS * V_DIM), BF16),
        scratch_shapes=[pltpu.VMEM((1, TQ), F32), pltpu.VMEM((1, TQ), F32), pltpu.VMEM((V_DIM, TQ), F32)],
        compiler_params=_params("parallel", "parallel", "arbitrary"),
        name="attention_online",
    )(q_t, kn, krope, v_t)


def _rglru_kernel(x_ref, g_ref, cw_ref, cb_ref, wa_ref, ba_ref, wx_ref, bx_ref, lam_ref,
                  y_ref, a_ref, b_ref, tail_ref, h_ref):
    t = pl.program_id(1)

    @pl.when(t == 0)
    def _():
        tail_ref[...] = jnp.zeros(tail_ref.shape, F32)
        h_ref[...] = jnp.zeros(h_ref.shape, F32)

    def group(n, carry):
        _rglru_group(n, x_ref, g_ref, cw_ref, cb_ref, wa_ref, ba_ref, wx_ref, bx_ref, lam_ref,
                     y_ref, a_ref, b_ref, tail_ref, h_ref)
        return carry

    lax.fori_loop(0, LRU_WIDTH // MXU_DIM, group, 0, unroll=2)


def _rglru_group(n, x_ref, g_ref, cw_ref, cb_ref, wa_ref, ba_ref, wx_ref, bx_ref, lam_ref,
                 y_ref, a_ref, b_ref, tail_ref, h_ref):
    R, L, W = LRU_CHUNKS, LRU_CHUNK_LEN, MXU_DIM
    slabs = W // LANES
    chunk = lax.broadcasted_iota(jnp.int32, (R, W), 0)

    def chunk_rows(ref, tau):
        return jnp.concatenate(
            [ref[n * slabs + j, pl.ds(tau, R, stride=L), :] for j in range(slabs)], axis=1)

    def step_rows(v, tau):
        return v[tau * R:(tau + 1) * R]

    def from_prev_chunk(v, first):
        return jnp.where(chunk == 0, first, pltpu.roll(v, 1, 0))

    x = jnp.concatenate([chunk_rows(x_ref, tau) for tau in range(L)], axis=0)
    tails = tail_ref[n]
    lead = [from_prev_chunk(step_rows(x, L - m), tails[m - 1:m, :]) for m in range(CONV_WIDTH - 1, 0, -1)]
    ext = jnp.concatenate(lead + [x], axis=0)
    tail_ref[n] = jnp.concatenate(
        [x[(L - m + 1) * R - 1:(L - m + 1) * R] for m in range(1, CONV_WIDTH)], axis=0)

    cw = cw_ref[n]
    xc = cb_ref[n] + ext[0:L * R] * cw[0:1, :]
    for tap in range(1, CONV_WIDTH):
        xc = xc + ext[tap * R:(tap + L) * R] * cw[tap:tap + 1, :]

    xcb = xc.astype(BF16)
    r = 1.0 / (1.0 + jnp.exp2(_dot(xcb, wa_ref[n]) + ba_ref[n]))
    gi = 1.0 / (1.0 + jnp.exp2(_dot(xcb, wx_ref[n]) + bx_ref[n]))

    lam = lam_ref[n]
    softplus_neg = jnp.maximum(-lam, 0.0) + jnp.log1p(jnp.exp(-jnp.abs(lam)))
    neg_log_a = r * (LRU_C * softplus_neg)
    a = jnp.exp2(neg_log_a * (-LOG2_E))
    m2 = jnp.tanh(neg_log_a) * (1.0 + a * a)
    mult = m2 * lax.rsqrt(jnp.maximum(m2, SQRT_FLOOR))
    a_ref[...] = a
    b_ref[...] = mult * (gi * xc)

    a_cum = a_ref[0:R, :]
    h_loc = b_ref[0:R, :]
    for tau in range(1, L):
        a_t = a_ref[tau * R:(tau + 1) * R, :]
        h_loc = a_t * h_loc + b_ref[tau * R:(tau + 1) * R, :]
        a_cum = a_t * a_cum
        a_ref[tau * R:(tau + 1) * R, :] = a_cum
        b_ref[tau * R:(tau + 1) * R, :] = h_loc

    carry = h_ref[n]
    sh = 1
    while sh < R:
        keep = chunk >= sh
        a_prev = jnp.where(keep, pltpu.roll(a_cum, sh, 0), 1.0)
        h_prev = jnp.where(keep, pltpu.roll(h_loc, sh, 0), 0.0)
        h_loc = a_cum * h_prev + h_loc
        a_cum = a_cum * a_prev
        sh *= 2
    h_end = a_cum * carry + h_loc
    h_in = from_prev_chunk(h_end, carry)
    h_ref[n] = h_end[R - 1:R, :]

    c0 = math.sqrt(2.0 / math.pi)
    for tau in range(L):
        h = b_ref[tau * R:(tau + 1) * R, :] + a_ref[tau * R:(tau + 1) * R, :] * h_in
        g = chunk_rows(g_ref, tau)
        half_g = 0.5 * g
        gelu = half_g + half_g * jnp.tanh(g * (c0 + (c0 * 0.044715) * (g * g)))
        y = h * gelu
        for j in range(slabs):
            y_ref[n * slabs + j, pl.ds(tau, R, stride=L), :] = y[:, j * LANES:(j + 1) * LANES]


def _rglru(x_lru, g_lru, conv_w, conv_b, wa_blk, ba, wx_blk, bx, lam, batch, seq):
    ts = LRU_CHUNKS * LRU_CHUNK_LEN
    nblk = seq // ts
    n_slab = LRU_WIDTH // LANES
    n_grp = LRU_WIDTH // MXU_DIM
    per_group = lambda v: v.reshape(-1, n_grp, MXU_DIM).transpose(1, 0, 2)
    conv_w, conv_b, ba, bx, lam = (per_group(v) for v in (conv_w, conv_b, ba, bx, lam))
    tile = pl.BlockSpec((n_slab, ts, LANES), lambda b, t: (0, b * nblk + t, 0))
    return pl.pallas_call(
        _rglru_kernel,
        grid=(batch, nblk),
        in_specs=[
            tile, tile,
            _const_spec(conv_w.shape), _const_spec(conv_b.shape),
            _const_spec(wa_blk.shape), _const_spec(ba.shape),
            _const_spec(wx_blk.shape), _const_spec(bx.shape), _const_spec(lam.shape),
        ],
        out_specs=tile,
        out_shape=jax.ShapeDtypeStruct((n_slab, batch * seq, LANES), F32),
        scratch_shapes=[
            pltpu.VMEM((ts, MXU_DIM), F32),
            pltpu.VMEM((ts, MXU_DIM), F32),
            pltpu.VMEM((n_grp, CONV_WIDTH - 1, MXU_DIM), F32),
            pltpu.VMEM((n_grp, 1, MXU_DIM), F32),
        ],
        compiler_params=_params("parallel", "arbitrary"),
        name="rglru",
    )(x_lru, g_lru, conv_w, conv_b, wa_blk, ba, wx_blk, bx, lam)


def _merge_kernel(x_ref, ya_ref, yb_ref, ga_ref, gb_ref, wpa_ref, wpl_ref, wo_ref, o_ref):
    pa = _dot(ya_ref[...], wpa_ref[...])
    pb = _dot(_load_slabs(yb_ref).astype(BF16), wpl_ref[...])
    merged = _sigmoid(ga_ref[...].astype(F32)) * pa + _sigmoid(gb_ref[...].astype(F32)) * pb
    o_ref[...] = x_ref[...] + _dot(merged.astype(BF16), wo_ref[...])


def _merge(x2d, ya, yb, ga, gb, wpa, wpl, wo):
    m = x2d.shape[0]
    row = lambda i: (i, 0)
    tile = pl.BlockSpec((TM_PROJ, D_MODEL), row)
    slabs = pl.BlockSpec((yb.shape[0], TM_PROJ, LANES), lambda i: (0, i, 0))
    return pl.pallas_call(
        _merge_kernel,
        grid=(m // TM_PROJ,),
        in_specs=[tile, tile, slabs, tile, tile]
        + [_const_spec(wpa.shape), _const_spec(wpl.shape), _const_spec(wo.shape)],
        out_specs=tile,
        out_shape=jax.ShapeDtypeStruct((m, D_MODEL), F32),
        compiler_params=_params("parallel"),
        name="merge",
    )(x2d, ya, yb, ga, gb, wpa, wpl, wo)


def _ffn_kernel(x_ref, g_ref, wg_ref, wu_ref, wd_ref, o_ref):
    x = x_ref[...]
    h = _rms_rows(x, g_ref[...]).astype(BF16)
    gate = _dot(h, wg_ref[...])
    up = _dot(h, wu_ref[...])
    act = (gate * _sigmoid(gate) * up).astype(BF16)
    o_ref[...] = x + _dot(act, wd_ref[...])


def _ffn(x2d, norm_g, wg, wu, wd):
    m = x2d.shape[0]
    tm = TM_FFN
    row = lambda i: (i, 0)
    tile = pl.BlockSpec((tm, D_MODEL), row)
    return pl.pallas_call(
        _ffn_kernel,
        grid=(m // tm,),
        in_specs=[tile, _const_spec(norm_g.shape), _const_spec(wg.shape), _const_spec(wu.shape),
                  _const_spec(wd.shape)],
        out_specs=tile,
        out_shape=jax.ShapeDtypeStruct((m, D_MODEL), F32),
        compiler_params=_params("parallel"),
        name="ffn",
    )(x2d, norm_g, wg, wu, wd)


def _block_diag_groups(w):
    per = MXU_DIM // LRU_BLOCK
    n_grp = w.shape[0] // per
    w4 = w.reshape(n_grp, per, LRU_BLOCK, LRU_BLOCK)
    eye = jnp.eye(per, dtype=w.dtype)
    out = jnp.einsum("gpde,pq->gpdqe", w4, eye)
    return out.reshape(n_grp, MXU_DIM, MXU_DIM)


def kernel(x, positions, norm_mix_g, w_in, q_lat_g, w_q_up, kv_lat_g, w_kv_up, q_head_g, k_head_g,
           conv_w, conv_b, lru_wa, lru_ba, lru_wx, lru_bx, lru_lambda, w_proj_attn, w_proj_lru,
           w_out, norm_ffn_g, w_ffn_gate, w_ffn_up, w_ffn_down):
    batch, seq, d = x.shape
    depth = w_in.shape[0]
    half = HALF_ROPE
    inv_freq = (ROPE_THETA ** (-jnp.arange(half, dtype=F32) / half)).reshape(half, 1)
    positions3 = positions.reshape(batch, 1, seq)
    x2d = x.reshape(batch * seq, d)

    for l in range(depth):
        w = w_in[l].astype(BF16)
        c0 = Q_RANK
        c1 = c0 + KV_RANK
        c2 = c1 + ROPE
        c3 = c2 + LRU_WIDTH
        c4 = c3 + LRU_WIDTH
        c5 = c4 + D_MODEL
        w_kr = jnp.pad(w[:, c1:c2], ((0, 0), (0, LANES - ROPE)))
        w_parts = [w[:, :c0], w[:, c0:c1], w_kr, w[:, c2:c3], w[:, c3:c4], w[:, c4:c5], w[:, c5:]]
        cq, ckv, kr, x_lru, g_lru, gate_a, gate_b = _in_proj(x2d, norm_mix_g[l].reshape(1, d), w_parts)

        wkv = w_kv_up[l].astype(BF16).reshape(KV_RANK, HEADS, NOPE + V_DIM)
        wkn = wkv[:, :, :NOPE].reshape(KV_RANK, HEADS * NOPE)
        wv_t = wkv[:, :, NOPE:].reshape(KV_RANK, HEADS * V_DIM).T
        wq_t = w_q_up[l].astype(BF16).T
        amax = lambda v: jnp.max(jnp.abs(v))
        gq, gk = q_head_g[l], k_head_g[l]
        bound2 = (QK_DIM ** -0.5 * LOG2_E) * (NOPE * amax(gq[:NOPE]) * amax(gk[:NOPE])
                                              + ROPE * amax(gq[NOPE:]) * amax(gk[NOPE:]))
        shift_ok = bound2 <= SHIFT_LIMIT_LOG2
        offset = jnp.where(shift_ok, -bound2, 0.0).reshape(1, 1)
        q_t, kn, krope, v_t = _mla_prep(
            cq, ckv, kr, positions3, inv_freq, offset,
            q_lat_g[l].reshape(1, Q_RANK), kv_lat_g[l].reshape(1, KV_RANK), wq_t, wkn, wv_t,
            q_head_g[l][:NOPE].reshape(NOPE, 1), q_head_g[l][NOPE:].reshape(ROPE, 1),
            k_head_g[l][:NOPE].reshape(1, NOPE), k_head_g[l][NOPE:].reshape(ROPE, 1),
            batch, seq)
        y_a = lax.cond(
            shift_ok,
            functools.partial(_attention_shifted, batch=batch, seq=seq),
            functools.partial(_attention_online, batch=batch, seq=seq),
            q_t, kn, krope, v_t).reshape(batch * seq, HEADS * V_DIM)

        y_b = _rglru(
            x_lru, g_lru, conv_w[l], conv_b[l].reshape(1, LRU_WIDTH),
            _block_diag_groups(lru_wa[l] * -LOG2_E).astype(BF16), (lru_ba[l] * -LOG2_E).reshape(1, LRU_WIDTH),
            _block_diag_groups(lru_wx[l] * -LOG2_E).astype(BF16), (lru_bx[l] * -LOG2_E).reshape(1, LRU_WIDTH),
            lru_lambda[l].reshape(1, LRU_WIDTH), batch, seq)

        x2d = _merge(x2d, y_a, y_b, gate_a, gate_b, w_proj_attn[l].astype(BF16),
                     w_proj_lru[l].astype(BF16), w_out[l].astype(BF16))
        x2d = _ffn(x2d, norm_ffn_g[l].reshape(1, d), w_ffn_gate[l].astype(BF16),
                   w_ffn_up[l].astype(BF16), w_ffn_down[l].astype(BF16))
    return x2d.reshape(batch, seq, d)
```

```python
import functools
import math

import jax
import jax.numpy as jnp
from jax import lax
from jax.experimental import pallas as pl
from jax.experimental.pallas import tpu as pltpu

D_MODEL = 1024
HEADS = 8
NOPE = 128
ROPE = 64
HALF_ROPE = ROPE // 2
QK_DIM = NOPE + ROPE
V_DIM = 128
Q_RANK = 256
KV_RANK = 256
ROPE_THETA = 10000.0
LRU_WIDTH = 1024
LRU_BLOCK = 64
CONV_WIDTH = 4
LRU_C = 8.0
EPS = 1e-6

LANES = 128
SUBLANES = 8
MXU_DIM = 256
QK_PAD = 2 * LANES
SUM_ROWS = 16

LOG2_E = math.log2(math.e)
Q_SCALE = QK_DIM ** -0.5 * LOG2_E
SHIFT_LIMIT_LOG2 = 60.0

VMEM_LIMIT = 56 * 1024 * 1024

TM_PROJ = 1024
TM_FFN = 512
TM_PREP = 512
TQ = 512
TK = 512
LRU_CHUNKS = 16
LRU_CHUNK_LEN = 16
SQRT_FLOOR = 1e-30
PAIRS_PER_TRIP = 34

F32 = jnp.float32
BF16 = jnp.bfloat16


def _params(*semantics):
    return pltpu.CompilerParams(dimension_semantics=semantics, vmem_limit_bytes=VMEM_LIMIT)


def _const_spec(shape):
    zeros = (0,) * len(shape)
    return pl.BlockSpec(shape, lambda *_: zeros, pipeline_mode=pl.Buffered(1))


def _sigmoid(v):
    return 1.0 / (1.0 + jnp.exp(-v))


def _rms_rows(v, gain_row):
    ms = jnp.mean(v * v, axis=-1, keepdims=True)
    return v * lax.rsqrt(ms + EPS) * gain_row


def _rms_cols(v, gain_col):
    ms = jnp.mean(v * v, axis=0, keepdims=True)
    return v * lax.rsqrt(ms + EPS) * gain_col


def _dot(a, b):
    return jnp.dot(a, b, preferred_element_type=F32)


def _dot_nt(a, b):
    return lax.dot_general(a, b, (((1,), (1,)), ((), ())), preferred_element_type=F32)


IN_PROJ_WIDTHS = (Q_RANK, KV_RANK, LANES, LRU_WIDTH, LRU_WIDTH, D_MODEL, D_MODEL)


def _in_proj_kernel(x_ref, g_ref, w_ref, cq_ref, ckv_ref, kr_ref, xl_ref, gl_ref, ga_ref, gb_ref):
    h = _rms_rows(x_ref[...], g_ref[...]).astype(BF16)
    starts = [sum(IN_PROJ_WIDTHS[:k]) for k in range(len(IN_PROJ_WIDTHS) + 1)]
    part = lambda k: _dot(h, w_ref[:, starts[k]:starts[k + 1]])
    cq_ref[...] = part(0)
    ckv_ref[...] = part(1)
    kr_ref[...] = part(2)
    _store_slabs(xl_ref, part(3))
    _store_slabs(gl_ref, part(4))
    ga_ref[...] = part(5).astype(BF16)
    gb_ref[...] = part(6).astype(BF16)


def _store_slabs(ref, v):
    for j in range(ref.shape[0]):
        ref[j] = v[:, j * LANES:(j + 1) * LANES].astype(ref.dtype)


def _load_slabs(ref):
    return jnp.concatenate([ref[j] for j in range(ref.shape[0])], axis=1)


def _in_proj(x2d, norm_g, w_all):
    assert w_all.shape[1] == sum(IN_PROJ_WIDTHS)
    m = x2d.shape[0]
    row = lambda i: (i, 0)
    slab = lambda i: (0, i, 0)
    n_slab = LRU_WIDTH // LANES
    plain = lambda n, dt: (pl.BlockSpec((TM_PROJ, n), row), jax.ShapeDtypeStruct((m, n), dt))
    slabs = (pl.BlockSpec((n_slab, TM_PROJ, LANES), slab), jax.ShapeDtypeStruct((n_slab, m, LANES), F32))
    outs = [plain(Q_RANK, F32), plain(KV_RANK, F32), plain(LANES, F32), slabs, slabs,
            plain(D_MODEL, BF16), plain(D_MODEL, BF16)]
    return pl.pallas_call(
        _in_proj_kernel,
        grid=(m // TM_PROJ,),
        in_specs=[pl.BlockSpec((TM_PROJ, D_MODEL), row), _const_spec((1, D_MODEL)), _const_spec(w_all.shape)],
        out_specs=[o[0] for o in outs],
        out_shape=[o[1] for o in outs],
        compiler_params=_params("parallel"),
        name="in_proj",
    )(x2d, norm_g, w_all)


def _mla_prep_kernel(cq_ref, ckv_ref, kr_ref, pos_ref, freq_ref, off_ref, qlg_ref, kvlg_ref,
                     wqt_ref, wkn_ref, wvt_ref, gqn_ref, gqr_ref, gkn_ref, gkr_ref,
                     qt_ref, kn_ref, krope_ref, vt_ref):
    tm = cq_ref.shape[0]
    pad_row = lax.broadcasted_iota(jnp.int32, (QK_PAD - QK_DIM, tm), 0)
    ang = freq_ref[...] * pos_ref[0].astype(F32)
    cos = jnp.cos(ang)
    sin = jnp.sin(ang)

    def rope_cols(v):
        x1, x2 = v[:HALF_ROPE], v[HALF_ROPE:]
        return x1 * cos - x2 * sin, x2 * cos + x1 * sin

    cqn = _rms_rows(cq_ref[...], qlg_ref[...]).astype(BF16)
    q_t = _dot_nt(wqt_ref[...], cqn)
    pad = jnp.where(pad_row == 0, off_ref[...], 0.0).astype(BF16)
    for h in range(HEADS):
        base = h * QK_DIM
        qn = _rms_cols(q_t[base:base + NOPE], gqn_ref[...])
        qr = _rms_cols(q_t[base + NOPE:base + QK_DIM], gqr_ref[...])
        o1, o2 = rope_cols(qr)
        q_pad = jnp.concatenate([qn.astype(BF16), o1.astype(BF16), o2.astype(BF16), pad], axis=0)
        for c in range(qt_ref.shape[2]):
            qt_ref[0, h, c] = q_pad[:, c * TQ:(c + 1) * TQ]

    ckvn = _rms_rows(ckv_ref[...], kvlg_ref[...]).astype(BF16)
    kn_all = _dot(ckvn, wkn_ref[...])
    v_t = _dot_nt(wvt_ref[...], ckvn)
    n_chunks = vt_ref.shape[2]
    for h in range(HEADS):
        kn = _rms_rows(kn_all[:, h * NOPE:(h + 1) * NOPE], gkn_ref[...])
        kn_ref[0, h] = kn.astype(BF16)
        for c in range(n_chunks):
            vt_ref[0, h, c] = v_t[h * V_DIM:(h + 1) * V_DIM, c * TK:(c + 1) * TK].astype(BF16)

    kr_t = kr_ref[...].T
    krn = _rms_cols(kr_t[:ROPE], gkr_ref[...])
    o1, o2 = rope_cols(krn)
    kr_out = jnp.concatenate([o1, o2, jnp.where(pad_row == 0, 1.0, 0.0)], axis=0)
    krope_ref[0] = kr_out.T.astype(BF16)


def _mla_prep(cq, ckv, kr, positions3, inv_freq, offset, q_lat_g, kv_lat_g, wq_t, wkn, wv_t,
              gqn, gqr, gkn, gkr, batch, seq):
    tm = TM_PREP
    nblk = seq // tm
    tok = lambda b, i: (b * nblk + i, 0)
    in_specs = [
        pl.BlockSpec((tm, Q_RANK), tok),
        pl.BlockSpec((tm, KV_RANK), tok),
        pl.BlockSpec((tm, LANES), tok),
        pl.BlockSpec((1, 1, tm), lambda b, i: (b, 0, i)),
        _const_spec(inv_freq.shape), _const_spec(offset.shape),
        _const_spec(q_lat_g.shape), _const_spec(kv_lat_g.shape),
        _const_spec(wq_t.shape), _const_spec(wkn.shape), _const_spec(wv_t.shape),
        _const_spec(gqn.shape), _const_spec(gqr.shape), _const_spec(gkn.shape), _const_spec(gkr.shape),
    ]
    out_specs = [
        pl.BlockSpec((1, HEADS, tm // TQ, QK_PAD, TQ), lambda b, i: (b, 0, i, 0, 0)),
        pl.BlockSpec((1, HEADS, tm, NOPE), lambda b, i: (b, 0, i, 0)),
        pl.BlockSpec((1, tm, LANES), lambda b, i: (b, i, 0)),
        pl.BlockSpec((1, HEADS, tm // TK, V_DIM, TK), lambda b, i: (b, 0, i, 0, 0)),
    ]
    out_shape = [
        jax.ShapeDtypeStruct((batch, HEADS, seq // TQ, QK_PAD, TQ), BF16),
        jax.ShapeDtypeStruct((batch, HEADS, seq, NOPE), BF16),
        jax.ShapeDtypeStruct((batch, seq, LANES), BF16),
        jax.ShapeDtypeStruct((batch, HEADS, seq // TK, V_DIM, TK), BF16),
    ]
    return pl.pallas_call(
        _mla_prep_kernel,
        grid=(batch, nblk),
        in_specs=in_specs,
        out_specs=out_specs,
        out_shape=out_shape,
        compiler_params=_params("parallel", "parallel"),
        name="mla_prep",
    )(cq, ckv, kr, positions3, inv_freq, offset, q_lat_g, kv_lat_g, wq_t, wkn, wv_t, gqn, gqr, gkn, gkr)


def _scores_t(qt_ref, kn_ref, kr_ref, i, j):
    start = pl.multiple_of(j * TK, TK)
    k = jnp.concatenate([kn_ref[0, 0, pl.ds(start, TK), :], kr_ref[0, pl.ds(start, TK), :]], axis=1)
    return _dot(k, qt_ref[0, 0, i])


def _causal_mask(s, i, j):
    key = j * TK + lax.broadcasted_iota(jnp.int32, s.shape, 0)
    qry = i * TQ + lax.broadcasted_iota(jnp.int32, s.shape, 1)
    return jnp.where(key <= qry, s, -jnp.inf)


def _attention_online_kernel(qt_ref, kn_ref, kr_ref, vt_ref, o_ref, m_ref, l_ref, acc_ref):
    i = pl.program_id(2)
    m_ref[...] = jnp.full(m_ref.shape, -jnp.inf, F32)
    l_ref[...] = jnp.zeros(l_ref.shape, F32)
    acc_ref[...] = jnp.zeros(acc_ref.shape, F32)

    def step(j, masked):
        s = _scores_t(qt_ref, kn_ref, kr_ref, 0, j)
        if masked:
            s = _causal_mask(s, i, j)
        m_old = m_ref[...]
        m_new = jnp.maximum(m_old, jnp.max(s, axis=0, keepdims=True))
        alpha = jnp.exp2(m_old - m_new)
        p = jnp.exp2(s - m_new)
        l_ref[...] = alpha * l_ref[...] + jnp.sum(p, axis=0, keepdims=True)
        acc_ref[...] = alpha * acc_ref[...] + _dot(vt_ref[0, 0, j], p.astype(BF16))
        m_ref[...] = m_new

    def body(j, carry):
        step(j, masked=False)
        return carry

    lax.fori_loop(0, i, body, 0)
    step(i, masked=True)
    out_t = acc_ref[...] / l_ref[...]
    o_ref[0] = out_t.T.astype(o_ref.dtype)


def _attention_shifted_kernel(itab_ref, jtab_ref, qt_ref, kn_ref, kr_ref, vt_ref, o_ref,
                              s0_ref, s1_ref, p0_ref, p1_ref, acc_ref, *, n_pairs, diag_per_trip):
    ones = jnp.ones((SUM_ROWS, TK), BF16)
    s_slots = (s0_ref, s1_ref)
    p_slots = (p0_ref, p1_ref)
    half = TK // 2
    assert TQ == TK and half % LANES == 0 and diag_per_trip >= 2

    def scores(t, s_ref, diagonal):
        if not diagonal:
            s_ref[...] = _scores_t(qt_ref, kn_ref, kr_ref, itab_ref[t], jtab_ref[t])
            return
        start = pl.multiple_of(jtab_ref[t] * TK, TK)
        k = jnp.concatenate([kn_ref[0, 0, pl.ds(start, TK), :], kr_ref[0, pl.ds(start, TK), :]], axis=1)
        q_t = qt_ref[0, 0, itab_ref[t]]
        s_ref[0:half, :] = _dot(k[0:half], q_t)
        s_ref[half:, half:] = _dot(k[half:], q_t[:, half:])

    def probs(t, s_ref, p_ref, diagonal):
        if not diagonal:
            p_ref[...] = jnp.exp2(s_ref[...]).astype(BF16)
            return

        def masked_exp2(s):
            key = lax.broadcasted_iota(jnp.int32, s.shape, 0)
            qry = lax.broadcasted_iota(jnp.int32, s.shape, 1)
            return jnp.exp2(jnp.where(key <= qry, s, -jnp.inf)).astype(BF16)

        p_ref[0:half, :] = masked_exp2(s_ref[0:half, :])
        p_ref[half:, half:] = masked_exp2(s_ref[half:, half:])

    def values(t, p_ref, diagonal):
        i = itab_ref[t]
        v_aug = jnp.concatenate([vt_ref[0, 0, jtab_ref[t]], ones], axis=0)
        if not diagonal:
            acc_ref[i] += _dot(v_aug, p_ref[...])
            return
        acc_ref[i, :, 0:half] += _dot(v_aug[:, 0:half], p_ref[0:half, 0:half])
        acc_ref[i, :, half:] += _dot(v_aug, p_ref[:, half:])

    def is_diagonal(position):
        return position % PAIRS_PER_TRIP < diag_per_trip

    acc_ref[...] = jnp.zeros(acc_ref.shape, F32)
    scores(0, s0_ref, diagonal=True)
    scores(1, s1_ref, diagonal=True)
    probs(0, s0_ref, p0_ref, diagonal=True)

    def body(u, carry):
        for sub in range(PAIRS_PER_TRIP):
            t = PAIRS_PER_TRIP * u + sub
            slot = sub % 2
            scores(t + 2, s_slots[slot], is_diagonal(sub + 2))
            probs(t + 1, s_slots[1 - slot], p_slots[1 - slot], is_diagonal(sub + 1))
            values(t, p_slots[slot], is_diagonal(sub))
        return carry

    lax.fori_loop(0, n_pairs // PAIRS_PER_TRIP, body, 0)

    for i in range(acc_ref.shape[0]):
        out_t = acc_ref[i, 0:V_DIM, :] / acc_ref[i, V_DIM:V_DIM + 1, :]
        o_ref[0, i * TQ:(i + 1) * TQ, :] = out_t.T.astype(o_ref.dtype)


def _attention_shifted(q_t, kn, krope, v_t, batch, seq):
    nq = seq // TQ
    diag = [(i, i) for i in range(nq)]
    below = [(i, j) for i in range(nq) for j in range(i)]
    n_pairs = len(diag) + len(below)
    trips = n_pairs // PAIRS_PER_TRIP
    assert TQ == TK and n_pairs % PAIRS_PER_TRIP == 0 and PAIRS_PER_TRIP % 2 == 0 and nq % trips == 0
    diag_per_trip = nq // trips
    below_per_trip = len(below) // trips
    pairs = []
    for u in range(trips):
        pairs += diag[u * diag_per_trip:(u + 1) * diag_per_trip]
        pairs += below[u * below_per_trip:(u + 1) * below_per_trip]
    pairs = pairs + [pairs[-1]] * 2
    itab = jnp.asarray([p[0] for p in pairs], jnp.int32)
    jtab = jnp.asarray([p[1] for p in pairs], jnp.int32)
    grid_spec = pltpu.PrefetchScalarGridSpec(
        num_scalar_prefetch=2,
        grid=(batch, HEADS),
        in_specs=[
            pl.BlockSpec((1, 1, nq, QK_PAD, TQ), lambda b, h, *_: (b, h, 0, 0, 0)),
            pl.BlockSpec((1, 1, seq, NOPE), lambda b, h, *_: (b, h, 0, 0)),
            pl.BlockSpec((1, seq, LANES), lambda b, h, *_: (b, 0, 0)),
            pl.BlockSpec((1, 1, seq // TK, V_DIM, TK), lambda b, h, *_: (b, h, 0, 0, 0)),
        ],
        out_specs=pl.BlockSpec((1, seq, V_DIM), lambda b, h, *_: (b, 0, h)),
        scratch_shapes=[
            pltpu.VMEM((TK, TQ), F32), pltpu.VMEM((TK, TQ), F32),
            pltpu.VMEM((TK, TQ), BF16), pltpu.VMEM((TK, TQ), BF16),
            pltpu.VMEM((nq, V_DIM + SUM_ROWS, TQ), F32),
        ],
    )
    return pl.pallas_call(
        functools.partial(_attention_shifted_kernel, n_pairs=n_pairs, diag_per_trip=diag_per_trip),
        grid_spec=grid_spec,
        out_shape=jax.ShapeDtypeStruct((batch, seq, HEADS * V_DIM), BF16),
        compiler_params=_params("parallel", "parallel"),
        name="attention_shifted",
    )(itab, jtab, q_t, kn, krope, v_t)


def _attention_online(q_t, kn, krope, v_t, batch, seq):
    assert TQ == TK
    nq = seq // TQ
    return pl.pallas_call(
        _attention_online_kernel,
        grid=(batch, HEADS, nq),
        in_specs=[
            pl.BlockSpec((1, 1, 1, QK_PAD, TQ), lambda b, h, i: (b, h, i, 0, 0)),
            pl.BlockSpec((1, 1, seq, NOPE), lambda b, h, i: (b, h, 0, 0)),
            pl.BlockSpec((1, seq, LANES), lambda b, h, i: (b, 0, 0)),
            pl.BlockSpec((1, 1, seq // TK, V_DIM, TK), lambda b, h, i: (b, h, 0, 0, 0)),
        ],
        out_specs=pl.BlockSpec((1, TQ, V_DIM), lambda b, h, i: (b, i, h)),
        out_shape=jax.ShapeDtypeStruct((batch, seq, HEADS * V_DIM), BF16),
        scratch_shapes=[pltpu.VMEM((1, TQ), F32), pltpu.VMEM((1, TQ), F32), pltpu.VMEM((V_DIM, TQ), F32)],
        compiler_params=_params("parallel", "parallel", "arbitrary"),
        name="attention_online",
    )(q_t, kn, krope, v_t)


def _rglru_kernel(x_ref, g_ref, cw_ref, cb_ref, wa_ref, ba_ref, wx_ref, bx_ref, lam_ref,
                  y_ref, a_ref, b_ref, tail_ref, h_ref):
    t = pl.program_id(1)

    @pl.when(t == 0)
    def _():
        tail_ref[...] = jnp.zeros(tail_ref.shape, F32)
        h_ref[...] = jnp.zeros(h_ref.shape, F32)

    def group(n, carry):
        _rglru_group(n, x_ref, g_ref, cw_ref, cb_ref, wa_ref, ba_ref, wx_ref, bx_ref, lam_ref,
                     y_ref, a_ref, b_ref, tail_ref, h_ref)
        return carry

    lax.fori_loop(0, LRU_WIDTH // MXU_DIM, group, 0, unroll=True)


def _rglru_group(n, x_ref, g_ref, cw_ref, cb_ref, wa_ref, ba_ref, wx_ref, bx_ref, lam_ref,
                 y_ref, a_ref, b_ref, tail_ref, h_ref):
    R, L, W = LRU_CHUNKS, LRU_CHUNK_LEN, MXU_DIM
    slabs = W // LANES
    chunk = lax.broadcasted_iota(jnp.int32, (R, W), 0)

    def chunk_rows(ref, tau):
        return jnp.concatenate(
            [ref[n * slabs + j, pl.ds(tau, R, stride=L), :] for j in range(slabs)], axis=1)

    def step_rows(v, tau):
        return v[tau * R:(tau + 1) * R]

    def from_prev_chunk(v, first):
        return jnp.where(chunk == 0, first, pltpu.roll(v, 1, 0))

    x = jnp.concatenate([chunk_rows(x_ref, tau) for tau in range(L)], axis=0)
    tails = tail_ref[n]
    lead = [from_prev_chunk(step_rows(x, L - m), tails[m - 1:m, :]) for m in range(CONV_WIDTH - 1, 0, -1)]
    ext = jnp.concatenate(lead + [x], axis=0)
    tail_ref[n] = jnp.concatenate(
        [x[(L - m + 1) * R - 1:(L - m + 1) * R] for m in range(1, CONV_WIDTH)], axis=0)

    cw = cw_ref[n]
    xc = cb_ref[n] + ext[0:L * R] * cw[0:1, :]
    for tap in range(1, CONV_WIDTH):
        xc = xc + ext[tap * R:(tap + L) * R] * cw[tap:tap + 1, :]

    xcb = xc.astype(BF16)
    r = 1.0 / (1.0 + jnp.exp2(_dot(xcb, wa_ref[n]) + ba_ref[n]))
    gi = 1.0 / (1.0 + jnp.exp2(_dot(xcb, wx_ref[n]) + bx_ref[n]))

    lam = lam_ref[n]
    softplus_neg = jnp.maximum(-lam, 0.0) + jnp.log1p(jnp.exp(-jnp.abs(lam)))
    neg_log_a = r * (LRU_C * softplus_neg)
    a = jnp.exp2(neg_log_a * (-LOG2_E))
    m2 = jnp.tanh(neg_log_a) * (1.0 + a * a)
    mult = m2 * lax.rsqrt(jnp.maximum(m2, SQRT_FLOOR))
    a_ref[...] = a
    b_ref[...] = mult * (gi * xc)

    a_cum = a_ref[0:R, :]
    h_loc = b_ref[0:R, :]
    for tau in range(1, L):
        a_t = a_ref[tau * R:(tau + 1) * R, :]
        h_loc = a_t * h_loc + b_ref[tau * R:(tau + 1) * R, :]
        a_cum = a_t * a_cum
        a_ref[tau * R:(tau + 1) * R, :] = a_cum
        b_ref[tau * R:(tau + 1) * R, :] = h_loc

    carry = h_ref[n]
    sh = 1
    while sh < R:
        keep = chunk >= sh
        a_prev = jnp.where(keep, pltpu.roll(a_cum, sh, 0), 1.0)
        h_prev = jnp.where(keep, pltpu.roll(h_loc, sh, 0), 0.0)
        h_loc = a_cum * h_prev + h_loc
        a_cum = a_cum * a_prev
        sh *= 2
    h_end = a_cum * carry + h_loc
    h_in = from_prev_chunk(h_end, carry)
    h_ref[n] = h_end[R - 1:R, :]

    c0 = math.sqrt(2.0 / math.pi)
    for tau in range(L):
        h = b_ref[tau * R:(tau + 1) * R, :] + a_ref[tau * R:(tau + 1) * R, :] * h_in
        g = chunk_rows(g_ref, tau)
        half_g = 0.5 * g
        gelu = half_g + half_g * jnp.tanh(g * (c0 + (c0 * 0.044715) * (g * g)))
        y = h * gelu
        for j in range(slabs):
            y_ref[n * slabs + j, pl.ds(tau, R, stride=L), :] = y[:, j * LANES:(j + 1) * LANES]


def _rglru(x_lru, g_lru, conv_w, conv_b, wa_blk, ba, wx_blk, bx, lam, batch, seq):
    ts = LRU_CHUNKS * LRU_CHUNK_LEN
    nblk = seq // ts
    n_slab = LRU_WIDTH // LANES
    n_grp = LRU_WIDTH // MXU_DIM
    per_group = lambda v: v.reshape(-1, n_grp, MXU_DIM).transpose(1, 0, 2)
    conv_w, conv_b, ba, bx, lam = (per_group(v) for v in (conv_w, conv_b, ba, bx, lam))
    tile = pl.BlockSpec((n_slab, ts, LANES), lambda b, t: (0, b * nblk + t, 0))
    return pl.pallas_call(
        _rglru_kernel,
        grid=(batch, nblk),
        in_specs=[
            tile, tile,
            _const_spec(conv_w.shape), _const_spec(conv_b.shape),
            _const_spec(wa_blk.shape), _const_spec(ba.shape),
            _const_spec(wx_blk.shape), _const_spec(bx.shape), _const_spec(lam.shape),
        ],
        out_specs=tile,
        out_shape=jax.ShapeDtypeStruct((n_slab, batch * seq, LANES), F32),
        scratch_shapes=[
            pltpu.VMEM((ts, MXU_DIM), F32),
            pltpu.VMEM((ts, MXU_DIM), F32),
            pltpu.VMEM((n_grp, CONV_WIDTH - 1, MXU_DIM), F32),
            pltpu.VMEM((n_grp, 1, MXU_DIM), F32),
        ],
        compiler_params=_params("parallel", "arbitrary"),
        name="rglru",
    )(x_lru, g_lru, conv_w, conv_b, wa_blk, ba, wx_blk, bx, lam)


def _merge_kernel(x_ref, ya_ref, yb_ref, ga_ref, gb_ref, wpa_ref, wpl_ref, wo_ref, o_ref):
    pa = _dot(ya_ref[...], wpa_ref[...])
    pb = _dot(_load_slabs(yb_ref).astype(BF16), wpl_ref[...])
    merged = _sigmoid(ga_ref[...].astype(F32)) * pa + _sigmoid(gb_ref[...].astype(F32)) * pb
    o_ref[...] = x_ref[...] + _dot(merged.astype(BF16), wo_ref[...])


def _merge(x2d, ya, yb, ga, gb, wpa, wpl, wo):
    m = x2d.shape[0]
    row = lambda i: (i, 0)
    tile = pl.BlockSpec((TM_PROJ, D_MODEL), row)
    slabs = pl.BlockSpec((yb.shape[0], TM_PROJ, LANES), lambda i: (0, i, 0))
    return pl.pallas_call(
        _merge_kernel,
        grid=(m // TM_PROJ,),
        in_specs=[tile, tile, slabs, tile, tile]
        + [_const_spec(wpa.shape), _const_spec(wpl.shape), _const_spec(wo.shape)],
        out_specs=tile,
        out_shape=jax.ShapeDtypeStruct((m, D_MODEL), F32),
        compiler_params=_params("parallel"),
        name="merge",
    )(x2d, ya, yb, ga, gb, wpa, wpl, wo)


def _ffn_kernel(x_ref, g_ref, wg_ref, wu_ref, wd_ref, o_ref):
    x = x_ref[...]
    h = _rms_rows(x, g_ref[...]).astype(BF16)
    gate = _dot(h, wg_ref[...])
    up = _dot(h, wu_ref[...])
    act = (gate * _sigmoid(gate) * up).astype(BF16)
    o_ref[...] = x + _dot(act, wd_ref[...])


def _ffn(x2d, norm_g, wg, wu, wd):
    m = x2d.shape[0]
    tm = TM_FFN
    row = lambda i: (i, 0)
    tile = pl.BlockSpec((tm, D_MODEL), row)
    return pl.pallas_call(
        _ffn_kernel,
        grid=(m // tm,),
        in_specs=[tile, _const_spec(norm_g.shape), _const_spec(wg.shape), _const_spec(wu.shape),
                  _const_spec(wd.shape)],
        out_specs=tile,
        out_shape=jax.ShapeDtypeStruct((m, D_MODEL), F32),
        compiler_params=_params("parallel"),
        name="ffn",
    )(x2d, norm_g, wg, wu, wd)


def _block_diag_groups(w):
    per = MXU_DIM // LRU_BLOCK
    n_grp = w.shape[0] // per
    w4 = w.reshape(n_grp, per, LRU_BLOCK, LRU_BLOCK)
    eye = jnp.eye(per, dtype=w.dtype)
    out = jnp.einsum("gpde,pq->gpdqe", w4, eye)
    return out.reshape(n_grp, MXU_DIM, MXU_DIM)


def kernel(x, positions, norm_mix_g, w_in, q_lat_g, w_q_up, kv_lat_g, w_kv_up, q_head_g, k_head_g,
           conv_w, conv_b, lru_wa, lru_ba, lru_wx, lru_bx, lru_lambda, w_proj_attn, w_proj_lru,
           w_out, norm_ffn_g, w_ffn_gate, w_ffn_up, w_ffn_down):
    batch, seq, d = x.shape
    depth = w_in.shape[0]
    half = HALF_ROPE
    inv_freq = (ROPE_THETA ** (-jnp.arange(half, dtype=F32) / half)).reshape(half, 1)
    positions3 = positions.reshape(batch, 1, seq)
    x2d = x.reshape(batch * seq, d)

    for l in range(depth):
        c_rope = Q_RANK + KV_RANK
        w_all = jnp.concatenate(
            [w_in[l][:, :c_rope], jnp.pad(w_in[l][:, c_rope:c_rope + ROPE], ((0, 0), (0, LANES - ROPE))),
             w_in[l][:, c_rope + ROPE:]], axis=1).astype(BF16)
        cq, ckv, kr, x_lru, g_lru, gate_a, gate_b = _in_proj(x2d, norm_mix_g[l].reshape(1, d), w_all)

        wkv = w_kv_up[l].astype(BF16).reshape(KV_RANK, HEADS, NOPE + V_DIM)
        wkn = wkv[:, :, :NOPE].reshape(KV_RANK, HEADS * NOPE)
        wv_t = wkv[:, :, NOPE:].reshape(KV_RANK, HEADS * V_DIM).T
        wq_t = w_q_up[l].astype(BF16).T
        amax = lambda v: jnp.max(jnp.abs(v))
        gq, gk = q_head_g[l], k_head_g[l]
        bound2 = Q_SCALE * (NOPE * amax(gq[:NOPE]) * amax(gk[:NOPE]) + ROPE * amax(gq[NOPE:]) * amax(gk[NOPE:]))
        shift_ok = bound2 <= SHIFT_LIMIT_LOG2
        offset = jnp.where(shift_ok, -bound2, 0.0).reshape(1, 1)
        q_t, kn, krope, v_t = _mla_prep(
            cq, ckv, kr, positions3, inv_freq, offset,
            q_lat_g[l].reshape(1, Q_RANK), kv_lat_g[l].reshape(1, KV_RANK), wq_t, wkn, wv_t,
            (gq[:NOPE] * Q_SCALE).reshape(NOPE, 1), (gq[NOPE:] * Q_SCALE).reshape(ROPE, 1),
            k_head_g[l][:NOPE].reshape(1, NOPE), k_head_g[l][NOPE:].reshape(ROPE, 1),
            batch, seq)
        y_a = lax.cond(
            shift_ok,
            functools.partial(_attention_shifted, batch=batch, seq=seq),
            functools.partial(_attention_online, batch=batch, seq=seq),
            q_t, kn, krope, v_t).reshape(batch * seq, HEADS * V_DIM)

        y_b = _rglru(
            x_lru, g_lru, conv_w[l], conv_b[l].reshape(1, LRU_WIDTH),
            _block_diag_groups(lru_wa[l] * -LOG2_E).astype(BF16), (lru_ba[l] * -LOG2_E).reshape(1, LRU_WIDTH),
            _block_diag_groups(lru_wx[l] * -LOG2_E).astype(BF16), (lru_bx[l] * -LOG2_E).reshape(1, LRU_WIDTH),
            lru_lambda[l].reshape(1, LRU_WIDTH), batch, seq)

        x2d = _merge(x2d, y_a, y_b, gate_a, gate_b, w_proj_attn[l].astype(BF16),
                     w_proj_lru[l].astype(BF16), w_out[l].astype(BF16))
        x2d = _ffn(x2d, norm_ffn_g[l].reshape(1, d), w_ffn_gate[l].astype(BF16),
                   w_ffn_up[l].astype(BF16), w_ffn_down[l].astype(BF16))
    return x2d.reshape(batch, seq, d)
```

```python
import functools
import math

import jax
import jax.numpy as jnp
from jax import lax
from jax.experimental import pallas as pl
from jax.experimental.pallas import tpu as pltpu

D_MODEL = 1024
HEADS = 8
NOPE = 128
ROPE = 64
HALF_ROPE = ROPE // 2
QK_DIM = NOPE + ROPE
V_DIM = 128
Q_RANK = 256
KV_RANK = 256
ROPE_THETA = 10000.0
LRU_WIDTH = 1024
LRU_BLOCK = 64
CONV_WIDTH = 4
LRU_C = 8.0
EPS = 1e-6

LANES = 128
SUBLANES = 8
MXU_DIM = 256
QK_PAD = 2 * LANES
SUM_ROWS = 16

LOG2_E = math.log2(math.e)
Q_SCALE = QK_DIM ** -0.5 * LOG2_E
SHIFT_LIMIT_LOG2 = 60.0

VMEM_LIMIT = 56 * 1024 * 1024

TM_PROJ = 1024
TM_FFN = 512
TM_PREP = 1024
TQ = 512
TK = 512
LRU_CHUNKS = 16
LRU_CHUNK_LEN = 16
SQRT_FLOOR = 1e-30
PAIRS_PER_TRIP = 34

F32 = jnp.float32
BF16 = jnp.bfloat16


def _params(*semantics):
    return pltpu.CompilerParams(dimension_semantics=semantics, vmem_limit_bytes=VMEM_LIMIT)


def _const_spec(shape):
    zeros = (0,) * len(shape)
    return pl.BlockSpec(shape, lambda *_: zeros, pipeline_mode=pl.Buffered(1))


def _sigmoid(v):
    return 1.0 / (1.0 + jnp.exp(-v))


def _rms_rows(v, gain_row):
    ms = jnp.mean(v * v, axis=-1, keepdims=True)
    return v * lax.rsqrt(ms + EPS) * gain_row


def _rms_cols(v, gain_col):
    ms = jnp.mean(v * v, axis=0, keepdims=True)
    return v * lax.rsqrt(ms + EPS) * gain_col


def _dot(a, b):
    return jnp.dot(a, b, preferred_element_type=F32)


def _dot_nt(a, b):
    return lax.dot_general(a, b, (((1,), (1,)), ((), ())), preferred_element_type=F32)


def _in_proj_kernel(x_ref, g_ref, wq_ref, wkv_ref, wkr_ref, wx_ref, wg_ref, wa_ref, wb_ref,
                    cq_ref, ckv_ref, kr_ref, xl_ref, gl_ref, ga_ref, gb_ref):
    h = _rms_rows(x_ref[...], g_ref[...]).astype(BF16)
    cq_ref[...] = _dot(h, wq_ref[...])
    ckv_ref[...] = _dot(h, wkv_ref[...])
    kr_ref[...] = _dot(h, wkr_ref[...])
    _store_slabs(xl_ref, _dot(h, wx_ref[...]))
    _store_slabs(gl_ref, _dot(h, wg_ref[...]))
    ga_ref[...] = _dot(h, wa_ref[...]).astype(BF16)
    gb_ref[...] = _dot(h, wb_ref[...]).astype(BF16)


def _store_slabs(ref, v):
    for j in range(ref.shape[0]):
        ref[j] = v[:, j * LANES:(j + 1) * LANES].astype(ref.dtype)


def _load_slabs(ref):
    return jnp.concatenate([ref[j] for j in range(ref.shape[0])], axis=1)


def _in_proj(x2d, norm_g, w_parts):
    m = x2d.shape[0]
    row = lambda i: (i, 0)
    slab = lambda i: (0, i, 0)
    n_slab = LRU_WIDTH // LANES
    plain = lambda n, dt: (pl.BlockSpec((TM_PROJ, n), row), jax.ShapeDtypeStruct((m, n), dt))
    slabs = (pl.BlockSpec((n_slab, TM_PROJ, LANES), slab), jax.ShapeDtypeStruct((n_slab, m, LANES), F32))
    outs = [plain(Q_RANK, F32), plain(KV_RANK, F32), plain(LANES, F32), slabs, slabs,
            plain(D_MODEL, BF16), plain(D_MODEL, BF16)]
    return pl.pallas_call(
        _in_proj_kernel,
        grid=(m // TM_PROJ,),
        in_specs=[pl.BlockSpec((TM_PROJ, D_MODEL), row), _const_spec((1, D_MODEL))]
        + [_const_spec(w.shape) for w in w_parts],
        out_specs=[o[0] for o in outs],
        out_shape=[o[1] for o in outs],
        compiler_params=_params("parallel"),
        name="in_proj",
    )(x2d, norm_g, *w_parts)


def _mla_prep_kernel(cq_ref, ckv_ref, kr_ref, pos_ref, freq_ref, off_ref, qlg_ref, kvlg_ref,
                     wqt_ref, wkn_ref, wvt_ref, gqn_ref, gqr_ref, gkn_ref, gkr_ref,
                     qt_ref, kn_ref, krope_ref, vt_ref):
    tm = cq_ref.shape[0]
    pad_row = lax.broadcasted_iota(jnp.int32, (QK_PAD - QK_DIM, tm), 0)
    ang = freq_ref[...] * pos_ref[0].astype(F32)
    cos = jnp.cos(ang)
    sin = jnp.sin(ang)

    def rope_cols(v):
        x1, x2 = v[:HALF_ROPE], v[HALF_ROPE:]
        return x1 * cos - x2 * sin, x2 * cos + x1 * sin

    cqn = _rms_rows(cq_ref[...], qlg_ref[...]).astype(BF16)
    q_t = _dot_nt(wqt_ref[...], cqn)
    pad = jnp.where(pad_row == 0, off_ref[...], 0.0).astype(BF16)
    for h in range(HEADS):
        base = h * QK_DIM
        qn = _rms_cols(q_t[base:base + NOPE], gqn_ref[...])
        qr = _rms_cols(q_t[base + NOPE:base + QK_DIM], gqr_ref[...])
        o1, o2 = rope_cols(qr)
        q_pad = jnp.concatenate([qn.astype(BF16), o1.astype(BF16), o2.astype(BF16), pad], axis=0)
        for c in range(qt_ref.shape[2]):
            qt_ref[0, h, c] = q_pad[:, c * TQ:(c + 1) * TQ]

    ckvn = _rms_rows(ckv_ref[...], kvlg_ref[...]).astype(BF16)
    kn_all = _dot(ckvn, wkn_ref[...])
    v_t = _dot_nt(wvt_ref[...], ckvn)
    n_chunks = vt_ref.shape[2]
    for h in range(HEADS):
        kn = _rms_rows(kn_all[:, h * NOPE:(h + 1) * NOPE], gkn_ref[...])
        kn_ref[0, h] = kn.astype(BF16)
        for c in range(n_chunks):
            vt_ref[0, h, c] = v_t[h * V_DIM:(h + 1) * V_DIM, c * TK:(c + 1) * TK].astype(BF16)

    kr_t = kr_ref[...].T
    krn = _rms_cols(kr_t[:ROPE], gkr_ref[...])
    o1, o2 = rope_cols(krn)
    kr_out = jnp.concatenate([o1, o2, jnp.where(pad_row == 0, 1.0, 0.0)], axis=0)
    krope_ref[0] = kr_out.T.astype(BF16)


def _mla_prep(cq, ckv, kr, positions3, inv_freq, offset, q_lat_g, kv_lat_g, wq_t, wkn, wv_t,
              gqn, gqr, gkn, gkr, batch, seq):
    tm = TM_PREP
    nblk = seq // tm
    tok = lambda b, i: (b * nblk + i, 0)
    in_specs = [
        pl.BlockSpec((tm, Q_RANK), tok),
        pl.BlockSpec((tm, KV_RANK), tok),
        pl.BlockSpec((tm, LANES), tok),
        pl.BlockSpec((1, 1, tm), lambda b, i: (b, 0, i)),
        _const_spec(inv_freq.shape), _const_spec(offset.shape),
        _const_spec(q_lat_g.shape), _const_spec(kv_lat_g.shape),
        _const_spec(wq_t.shape), _const_spec(wkn.shape), _const_spec(wv_t.shape),
        _const_spec(gqn.shape), _const_spec(gqr.shape), _const_spec(gkn.shape), _const_spec(gkr.shape),
    ]
    out_specs = [
        pl.BlockSpec((1, HEADS, tm // TQ, QK_PAD, TQ), lambda b, i: (b, 0, i, 0, 0)),
        pl.BlockSpec((1, HEADS, tm, NOPE), lambda b, i: (b, 0, i, 0)),
        pl.BlockSpec((1, tm, LANES), lambda b, i: (b, i, 0)),
        pl.BlockSpec((1, HEADS, tm // TK, V_DIM, TK), lambda b, i: (b, 0, i, 0, 0)),
    ]
    out_shape = [
        jax.ShapeDtypeStruct((batch, HEADS, seq // TQ, QK_PAD, TQ), BF16),
        jax.ShapeDtypeStruct((batch, HEADS, seq, NOPE), BF16),
        jax.ShapeDtypeStruct((batch, seq, LANES), BF16),
        jax.ShapeDtypeStruct((batch, HEADS, seq // TK, V_DIM, TK), BF16),
    ]
    return pl.pallas_call(
        _mla_prep_kernel,
        grid=(batch, nblk),
        in_specs=in_specs,
        out_specs=out_specs,
        out_shape=out_shape,
        compiler_params=_params("parallel", "parallel"),
        name="mla_prep",
    )(cq, ckv, kr, positions3, inv_freq, offset, q_lat_g, kv_lat_g, wq_t, wkn, wv_t, gqn, gqr, gkn, gkr)


def _scores_t(qt_ref, kn_ref, kr_ref, i, j):
    start = pl.multiple_of(j * TK, TK)
    k = jnp.concatenate([kn_ref[0, 0, pl.ds(start, TK), :], kr_ref[0, pl.ds(start, TK), :]], axis=1)
    return _dot(k, qt_ref[0, 0, i])


def _causal_mask(s, i, j):
    key = j * TK + lax.broadcasted_iota(jnp.int32, s.shape, 0)
    qry = i * TQ + lax.broadcasted_iota(jnp.int32, s.shape, 1)
    return jnp.where(key <= qry, s, -jnp.inf)


def _attention_online_kernel(qt_ref, kn_ref, kr_ref, vt_ref, o_ref, m_ref, l_ref, acc_ref):
    i = pl.program_id(2)
    m_ref[...] = jnp.full(m_ref.shape, -jnp.inf, F32)
    l_ref[...] = jnp.zeros(l_ref.shape, F32)
    acc_ref[...] = jnp.zeros(acc_ref.shape, F32)

    def step(j, masked):
        s = _scores_t(qt_ref, kn_ref, kr_ref, 0, j)
        if masked:
            s = _causal_mask(s, i, j)
        m_old = m_ref[...]
        m_new = jnp.maximum(m_old, jnp.max(s, axis=0, keepdims=True))
        alpha = jnp.exp2(m_old - m_new)
        p = jnp.exp2(s - m_new)
        l_ref[...] = alpha * l_ref[...] + jnp.sum(p, axis=0, keepdims=True)
        acc_ref[...] = alpha * acc_ref[...] + _dot(vt_ref[0, 0, j], p.astype(BF16))
        m_ref[...] = m_new

    def body(j, carry):
        step(j, masked=False)
        return carry

    lax.fori_loop(0, i, body, 0)
    step(i, masked=True)
    out_t = acc_ref[...] / l_ref[...]
    o_ref[0] = out_t.T.astype(o_ref.dtype)


def _attention_shifted_kernel(itab_ref, jtab_ref, qt_ref, kn_ref, kr_ref, vt_ref, o_ref,
                              s0_ref, s1_ref, p0_ref, p1_ref, acc_ref, *, n_pairs, diag_per_trip):
    ones = jnp.ones((SUM_ROWS, TK), BF16)
    s_slots = (s0_ref, s1_ref)
    p_slots = (p0_ref, p1_ref)
    half = TK // 2
    assert TQ == TK and half % LANES == 0 and diag_per_trip >= 2

    def scores(t, s_ref, diagonal):
        if not diagonal:
            s_ref[...] = _scores_t(qt_ref, kn_ref, kr_ref, itab_ref[t], jtab_ref[t])
            return
        start = pl.multiple_of(jtab_ref[t] * TK, TK)
        k = jnp.concatenate([kn_ref[0, 0, pl.ds(start, TK), :], kr_ref[0, pl.ds(start, TK), :]], axis=1)
        q_t = qt_ref[0, 0, itab_ref[t]]
        s_ref[0:half, :] = _dot(k[0:half], q_t)
        s_ref[half:, half:] = _dot(k[half:], q_t[:, half:])

    def probs(t, s_ref, p_ref, diagonal):
        if not diagonal:
            p_ref[...] = jnp.exp2(s_ref[...]).astype(BF16)
            return

        def masked_exp2(s):
            key = lax.broadcasted_iota(jnp.int32, s.shape, 0)
            qry = lax.broadcasted_iota(jnp.int32, s.shape, 1)
            return jnp.exp2(jnp.where(key <= qry, s, -jnp.inf)).astype(BF16)

        p_ref[0:half, :] = masked_exp2(s_ref[0:half, :])
        p_ref[half:, half:] = masked_exp2(s_ref[half:, half:])

    def values(t, p_ref, diagonal):
        i = itab_ref[t]
        v_aug = jnp.concatenate([vt_ref[0, 0, jtab_ref[t]], ones], axis=0)
        if not diagonal:
            acc_ref[i] += _dot(v_aug, p_ref[...])
            return
        acc_ref[i, :, 0:half] += _dot(v_aug[:, 0:half], p_ref[0:half, 0:half])
        acc_ref[i, :, half:] += _dot(v_aug, p_ref[:, half:])

    def is_diagonal(position):
        return position % PAIRS_PER_TRIP < diag_per_trip

    acc_ref[...] = jnp.zeros(acc_ref.shape, F32)
    scores(0, s0_ref, diagonal=True)
    scores(1, s1_ref, diagonal=True)
    probs(0, s0_ref, p0_ref, diagonal=True)

    def body(u, carry):
        for sub in range(PAIRS_PER_TRIP):
            t = PAIRS_PER_TRIP * u + sub
            slot = sub % 2
            scores(t + 2, s_slots[slot], is_diagonal(sub + 2))
            probs(t + 1, s_slots[1 - slot], p_slots[1 - slot], is_diagonal(sub + 1))
            values(t, p_slots[slot], is_diagonal(sub))
        return carry

    lax.fori_loop(0, n_pairs // PAIRS_PER_TRIP, body, 0)

    for i in range(acc_ref.shape[0]):
        out_t = acc_ref[i, 0:V_DIM, :] / acc_ref[i, V_DIM:V_DIM + 1, :]
        o_ref[0, i * TQ:(i + 1) * TQ, :] = out_t.T.astype(o_ref.dtype)


def _attention_shifted(q_t, kn, krope, v_t, batch, seq):
    nq = seq // TQ
    diag = [(i, i) for i in range(nq)]
    below = [(i, j) for i in range(nq) for j in range(i)]
    n_pairs = len(diag) + len(below)
    trips = n_pairs // PAIRS_PER_TRIP
    assert TQ == TK and n_pairs % PAIRS_PER_TRIP == 0 and PAIRS_PER_TRIP % 2 == 0 and nq % trips == 0
    diag_per_trip = nq // trips
    below_per_trip = len(below) // trips
    pairs = []
    for u in range(trips):
        pairs += diag[u * diag_per_trip:(u + 1) * diag_per_trip]
        pairs += below[u * below_per_trip:(u + 1) * below_per_trip]
    pairs = pairs + [pairs[-1]] * 2
    itab = jnp.asarray([p[0] for p in pairs], jnp.int32)
    jtab = jnp.asarray([p[1] for p in pairs], jnp.int32)
    grid_spec = pltpu.PrefetchScalarGridSpec(
        num_scalar_prefetch=2,
        grid=(batch, HEADS),
        in_specs=[
            pl.BlockSpec((1, 1, nq, QK_PAD, TQ), lambda b, h, *_: (b, h, 0, 0, 0)),
            pl.BlockSpec((1, 1, seq, NOPE), lambda b, h, *_: (b, h, 0, 0)),
            pl.BlockSpec((1, seq, LANES), lambda b, h, *_: (b, 0, 0)),
            pl.BlockSpec((1, 1, seq // TK, V_DIM, TK), lambda b, h, *_: (b, h, 0, 0, 0)),
        ],
        out_specs=pl.BlockSpec((1, seq, V_DIM), lambda b, h, *_: (b, 0, h)),
        scratch_shapes=[
            pltpu.VMEM((TK, TQ), F32), pltpu.VMEM((TK, TQ), F32),
            pltpu.VMEM((TK, TQ), BF16), pltpu.VMEM((TK, TQ), BF16),
            pltpu.VMEM((nq, V_DIM + SUM_ROWS, TQ), F32),
        ],
    )
    return pl.pallas_call(
        functools.partial(_attention_shifted_kernel, n_pairs=n_pairs, diag_per_trip=diag_per_trip),
        grid_spec=grid_spec,
        out_shape=jax.ShapeDtypeStruct((batch, seq, HEADS * V_DIM), BF16),
        compiler_params=_params("parallel", "parallel"),
        name="attention_shifted",
    )(itab, jtab, q_t, kn, krope, v_t)


def _attention_online(q_t, kn, krope, v_t, batch, seq):
    assert TQ == TK
    nq = seq // TQ
    return pl.pallas_call(
        _attention_online_kernel,
        grid=(batch, HEADS, nq),
        in_specs=[
            pl.BlockSpec((1, 1, 1, QK_PAD, TQ), lambda b, h, i: (b, h, i, 0, 0)),
            pl.BlockSpec((1, 1, seq, NOPE), lambda b, h, i: (b, h, 0, 0)),
            pl.BlockSpec((1, seq, LANES), lambda b, h, i: (b, 0, 0)),
            pl.BlockSpec((1, 1, seq // TK, V_DIM, TK), lambda b, h, i: (b, h, 0, 0, 0)),
        ],
        out_specs=pl.BlockSpec((1, TQ, V_DIM), lambda b, h, i: (b, i, h)),
        out_shape=jax.ShapeDtypeStruct((batch, seq, HEADS * V_DIM), BF16),
        scratch_shapes=[pltpu.VMEM((1, TQ), F32), pltpu.VMEM((1, TQ), F32), pltpu.VMEM((V_DIM, TQ), F32)],
        compiler_params=_params("parallel", "parallel", "arbitrary"),
        name="attention_online",
    )(q_t, kn, krope, v_t)


def _rglru_kernel(x_ref, g_ref, cw_ref, cb_ref, wa_ref, ba_ref, wx_ref, bx_ref, lam_ref,
                  y_ref, a_ref, b_ref, tail_ref, h_ref):
    t = pl.program_id(1)

    @pl.when(t == 0)
    def _():
        tail_ref[...] = jnp.zeros(tail_ref.shape, F32)
        h_ref[...] = jnp.zeros(h_ref.shape, F32)

    def group(n, carry):
        _rglru_group(n, x_ref, g_ref, cw_ref, cb_ref, wa_ref, ba_ref, wx_ref, bx_ref, lam_ref,
                     y_ref, a_ref, b_ref, tail_ref, h_ref)
        return carry

    lax.fori_loop(0, LRU_WIDTH // MXU_DIM, group, 0, unroll=True)


def _rglru_group(n, x_ref, g_ref, cw_ref, cb_ref, wa_ref, ba_ref, wx_ref, bx_ref, lam_ref,
                 y_ref, a_ref, b_ref, tail_ref, h_ref):
    R, L, W = LRU_CHUNKS, LRU_CHUNK_LEN, MXU_DIM
    slabs = W // LANES
    chunk = lax.broadcasted_iota(jnp.int32, (R, W), 0)

    def chunk_rows(ref, tau):
        return jnp.concatenate(
            [ref[n * slabs + j, pl.ds(tau, R, stride=L), :] for j in range(slabs)], axis=1)

    def step_rows(v, tau):
        return v[tau * R:(tau + 1) * R]

    def from_prev_chunk(v, first):
        return jnp.where(chunk == 0, first, pltpu.roll(v, 1, 0))

    x = jnp.concatenate([chunk_rows(x_ref, tau) for tau in range(L)], axis=0)
    tails = tail_ref[n]
    lead = [from_prev_chunk(step_rows(x, L - m), tails[m - 1:m, :]) for m in range(CONV_WIDTH - 1, 0, -1)]
    ext = jnp.concatenate(lead + [x], axis=0)
    tail_ref[n] = jnp.concatenate(
        [x[(L - m + 1) * R - 1:(L - m + 1) * R] for m in range(1, CONV_WIDTH)], axis=0)

    cw = cw_ref[n]
    xc = cb_ref[n] + ext[0:L * R] * cw[0:1, :]
    for tap in range(1, CONV_WIDTH):
        xc = xc + ext[tap * R:(tap + L) * R] * cw[tap:tap + 1, :]

    xcb = xc.astype(BF16)
    r = 1.0 / (1.0 + jnp.exp2(_dot(xcb, wa_ref[n]) + ba_ref[n]))
    gi = 1.0 / (1.0 + jnp.exp2(_dot(xcb, wx_ref[n]) + bx_ref[n]))

    lam = lam_ref[n]
    softplus_neg = jnp.maximum(-lam, 0.0) + jnp.log1p(jnp.exp(-jnp.abs(lam)))
    neg_log_a = r * (LRU_C * softplus_neg)
    a = jnp.exp2(neg_log_a * (-LOG2_E))
    m2 = jnp.tanh(neg_log_a) * (1.0 + a * a)
    mult = m2 * lax.rsqrt(jnp.maximum(m2, SQRT_FLOOR))
    a_ref[...] = a
    b_ref[...] = mult * (gi * xc)

    a_cum = a_ref[0:R, :]
    h_loc = b_ref[0:R, :]
    for tau in range(1, L):
        a_t = a_ref[tau * R:(tau + 1) * R, :]
        h_loc = a_t * h_loc + b_ref[tau * R:(tau + 1) * R, :]
        a_cum = a_t * a_cum
        a_ref[tau * R:(tau + 1) * R, :] = a_cum
        b_ref[tau * R:(tau + 1) * R, :] = h_loc

    carry = h_ref[n]
    sh = 1
    while sh < R:
        keep = chunk >= sh
        a_prev = jnp.where(keep, pltpu.roll(a_cum, sh, 0), 1.0)
        h_prev = jnp.where(keep, pltpu.roll(h_loc, sh, 0), 0.0)
        h_loc = a_cum * h_prev + h_loc
        a_cum = a_cum * a_prev
        sh *= 2
    h_end = a_cum * carry + h_loc
    h_in = from_prev_chunk(h_end, carry)
    h_ref[n] = h_end[R - 1:R, :]

    c0 = math.sqrt(2.0 / math.pi)
    for tau in range(L):
        h = b_ref[tau * R:(tau + 1) * R, :] + a_ref[tau * R:(tau + 1) * R, :] * h_in
        g = chunk_rows(g_ref, tau)
        half_g = 0.5 * g
        gelu = half_g + half_g * jnp.tanh(g * (c0 + (c0 * 0.044715) * (g * g)))
        y = h * gelu
        for j in range(slabs):
            y_ref[n * slabs + j, pl.ds(tau, R, stride=L), :] = y[:, j * LANES:(j + 1) * LANES]


def _rglru(x_lru, g_lru, conv_w, conv_b, wa_blk, ba, wx_blk, bx, lam, batch, seq):
    ts = LRU_CHUNKS * LRU_CHUNK_LEN
    nblk = seq // ts
    n_slab = LRU_WIDTH // LANES
    n_grp = LRU_WIDTH // MXU_DIM
    per_group = lambda v: v.reshape(-1, n_grp, MXU_DIM).transpose(1, 0, 2)
    conv_w, conv_b, ba, bx, lam = (per_group(v) for v in (conv_w, conv_b, ba, bx, lam))
    tile = pl.BlockSpec((n_slab, ts, LANES), lambda b, t: (0, b * nblk + t, 0))
    return pl.pallas_call(
        _rglru_kernel,
        grid=(batch, nblk),
        in_specs=[
            tile, tile,
            _const_spec(conv_w.shape), _const_spec(conv_b.shape),
            _const_spec(wa_blk.shape), _const_spec(ba.shape),
            _const_spec(wx_blk.shape), _const_spec(bx.shape), _const_spec(lam.shape),
        ],
        out_specs=tile,
        out_shape=jax.ShapeDtypeStruct((n_slab, batch * seq, LANES), F32),
        scratch_shapes=[
            pltpu.VMEM((ts, MXU_DIM), F32),
            pltpu.VMEM((ts, MXU_DIM), F32),
            pltpu.VMEM((n_grp, CONV_WIDTH - 1, MXU_DIM), F32),
            pltpu.VMEM((n_grp, 1, MXU_DIM), F32),
        ],
        compiler_params=_params("parallel", "arbitrary"),
        name="rglru",
    )(x_lru, g_lru, conv_w, conv_b, wa_blk, ba, wx_blk, bx, lam)


def _merge_kernel(x_ref, ya_ref, yb_ref, ga_ref, gb_ref, wpa_ref, wpl_ref, wo_ref, o_ref):
    pa = _dot(ya_ref[...], wpa_ref[...])
    pb = _dot(_load_slabs(yb_ref).astype(BF16), wpl_ref[...])
    merged = _sigmoid(ga_ref[...].astype(F32)) * pa + _sigmoid(gb_ref[...].astype(F32)) * pb
    o_ref[...] = x_ref[...] + _dot(merged.astype(BF16), wo_ref[...])


def _merge(x2d, ya, yb, ga, gb, wpa, wpl, wo):
    m = x2d.shape[0]
    row = lambda i: (i, 0)
    tile = pl.BlockSpec((TM_PROJ, D_MODEL), row)
    slabs = pl.BlockSpec((yb.shape[0], TM_PROJ, LANES), lambda i: (0, i, 0))
    return pl.pallas_call(
        _merge_kernel,
        grid=(m // TM_PROJ,),
        in_specs=[tile, tile, slabs, tile, tile]
        + [_const_spec(wpa.shape), _const_spec(wpl.shape), _const_spec(wo.shape)],
        out_specs=tile,
        out_shape=jax.ShapeDtypeStruct((m, D_MODEL), F32),
        compiler_params=_params("parallel"),
        name="merge",
    )(x2d, ya, yb, ga, gb, wpa, wpl, wo)


def _ffn_kernel(x_ref, g_ref, wg_ref, wu_ref, wd_ref, o_ref):
    x = x_ref[...]
    h = _rms_rows(x, g_ref[...]).astype(BF16)
    gate = _dot(h, wg_ref[...])
    up = _dot(h, wu_ref[...])
    act = (gate * _sigmoid(gate) * up).astype(BF16)
    o_ref[...] = x + _dot(act, wd_ref[...])


def _ffn(x2d, norm_g, wg, wu, wd):
    m = x2d.shape[0]
    tm = TM_FFN
    row = lambda i: (i, 0)
    tile = pl.BlockSpec((tm, D_MODEL), row)
    return pl.pallas_call(
        _ffn_kernel,
        grid=(m // tm,),
        in_specs=[tile, _const_spec(norm_g.shape), _const_spec(wg.shape), _const_spec(wu.shape),
                  _const_spec(wd.shape)],
        out_specs=tile,
        out_shape=jax.ShapeDtypeStruct((m, D_MODEL), F32),
        compiler_params=_params("parallel"),
        name="ffn",
    )(x2d, norm_g, wg, wu, wd)


def _block_diag_groups(w):
    per = MXU_DIM // LRU_BLOCK
    n_grp = w.shape[0] // per
    w4 = w.reshape(n_grp, per, LRU_BLOCK, LRU_BLOCK)
    eye = jnp.eye(per, dtype=w.dtype)
    out = jnp.einsum("gpde,pq->gpdqe", w4, eye)
    return out.reshape(n_grp, MXU_DIM, MXU_DIM)


def kernel(x, positions, norm_mix_g, w_in, q_lat_g, w_q_up, kv_lat_g, w_kv_up, q_head_g, k_head_g,
           conv_w, conv_b, lru_wa, lru_ba, lru_wx, lru_bx, lru_lambda, w_proj_attn, w_proj_lru,
           w_out, norm_ffn_g, w_ffn_gate, w_ffn_up, w_ffn_down):
    batch, seq, d = x.shape
    depth = w_in.shape[0]
    half = HALF_ROPE
    inv_freq = (ROPE_THETA ** (-jnp.arange(half, dtype=F32) / half)).reshape(half, 1)
    positions3 = positions.reshape(batch, 1, seq)
    x2d = x.reshape(batch * seq, d)

    for l in range(depth):
        w = w_in[l].astype(BF16)
        c0 = Q_RANK
        c1 = c0 + KV_RANK
        c2 = c1 + ROPE
        c3 = c2 + LRU_WIDTH
        c4 = c3 + LRU_WIDTH
        c5 = c4 + D_MODEL
        w_kr = jnp.pad(w[:, c1:c2], ((0, 0), (0, LANES - ROPE)))
        w_parts = [w[:, :c0], w[:, c0:c1], w_kr, w[:, c2:c3], w[:, c3:c4], w[:, c4:c5], w[:, c5:]]
        cq, ckv, kr, x_lru, g_lru, gate_a, gate_b = _in_proj(x2d, norm_mix_g[l].reshape(1, d), w_parts)

        wkv = w_kv_up[l].astype(BF16).reshape(KV_RANK, HEADS, NOPE + V_DIM)
        wkn = wkv[:, :, :NOPE].reshape(KV_RANK, HEADS * NOPE)
        wv_t = wkv[:, :, NOPE:].reshape(KV_RANK, HEADS * V_DIM).T
        wq_t = w_q_up[l].astype(BF16).T
        amax = lambda v: jnp.max(jnp.abs(v))
        gq, gk = q_head_g[l], k_head_g[l]
        bound2 = Q_SCALE * (NOPE * amax(gq[:NOPE]) * amax(gk[:NOPE]) + ROPE * amax(gq[NOPE:]) * amax(gk[NOPE:]))
        shift_ok = bound2 <= SHIFT_LIMIT_LOG2
        offset = jnp.where(shift_ok, -bound2, 0.0).reshape(1, 1)
        q_t, kn, krope, v_t = _mla_prep(
            cq, ckv, kr, positions3, inv_freq, offset,
            q_lat_g[l].reshape(1, Q_RANK), kv_lat_g[l].reshape(1, KV_RANK), wq_t, wkn, wv_t,
            (gq[:NOPE] * Q_SCALE).reshape(NOPE, 1), (gq[NOPE:] * Q_SCALE).reshape(ROPE, 1),
            k_head_g[l][:NOPE].reshape(1, NOPE), k_head_g[l][NOPE:].reshape(ROPE, 1),
            batch, seq)
        y_a = lax.cond(
            shift_ok,
            functools.partial(_attention_shifted, batch=batch, seq=seq),
            functools.partial(_attention_online, batch=batch, seq=seq),
            q_t, kn, krope, v_t).reshape(batch * seq, HEADS * V_DIM)

        y_b = _rglru(
            x_lru, g_lru, conv_w[l], conv_b[l].reshape(1, LRU_WIDTH),
            _block_diag_groups(lru_wa[l] * -LOG2_E).astype(BF16), (lru_ba[l] * -LOG2_E).reshape(1, LRU_WIDTH),
            _block_diag_groups(lru_wx[l] * -LOG2_E).astype(BF16), (lru_bx[l] * -LOG2_E).reshape(1, LRU_WIDTH),
            lru_lambda[l].reshape(1, LRU_WIDTH), batch, seq)

        x2d = _merge(x2d, y_a, y_b, gate_a, gate_b, w_proj_attn[l].astype(BF16),
                     w_proj_lru[l].astype(BF16), w_out[l].astype(BF16))
        x2d = _ffn(x2d, norm_ffn_g[l].reshape(1, d), w_ffn_gate[l].astype(BF16),
                   w_ffn_up[l].astype(BF16), w_ffn_down[l].astype(BF16))
    return x2d.reshape(batch, seq, d)
```

```python
import functools
import math

import jax
import jax.numpy as jnp
from jax import lax
from jax.experimental import pallas as pl
from jax.experimental.pallas import tpu as pltpu

D_MODEL = 1024
HEADS = 8
NOPE = 128
ROPE = 64
HALF_ROPE = ROPE // 2
QK_DIM = NOPE + ROPE
V_DIM = 128
Q_RANK = 256
KV_RANK = 256
ROPE_THETA = 10000.0
LRU_WIDTH = 1024
LRU_BLOCK = 64
CONV_WIDTH = 4
LRU_C = 8.0
EPS = 1e-6

LANES = 128
SUBLANES = 8
MXU_DIM = 256
QK_PAD = 2 * LANES
SUM_ROWS = 16

LOG2_E = math.log2(math.e)
Q_SCALE = QK_DIM ** -0.5 * LOG2_E
SHIFT_LIMIT_LOG2 = 60.0

VMEM_LIMIT = 56 * 1024 * 1024

TM_PROJ = 1024
TM_FFN = 512
TM_PREP = 2048
TQ = 512
TK = 512
LRU_CHUNKS = 16
LRU_CHUNK_LEN = 16
SQRT_FLOOR = 1e-30
PAIRS_PER_TRIP = 34

F32 = jnp.float32
BF16 = jnp.bfloat16


def _params(*semantics):
    return pltpu.CompilerParams(dimension_semantics=semantics, vmem_limit_bytes=VMEM_LIMIT)


def _const_spec(shape):
    zeros = (0,) * len(shape)
    return pl.BlockSpec(shape, lambda *_: zeros, pipeline_mode=pl.Buffered(1))


def _sigmoid(v):
    return 1.0 / (1.0 + jnp.exp(-v))


def _rms_rows(v, gain_row):
    ms = jnp.mean(v * v, axis=-1, keepdims=True)
    return v * lax.rsqrt(ms + EPS) * gain_row


def _rms_cols(v, gain_col):
    ms = jnp.mean(v * v, axis=0, keepdims=True)
    return v * lax.rsqrt(ms + EPS) * gain_col


def _dot(a, b):
    return jnp.dot(a, b, preferred_element_type=F32)


def _dot_nt(a, b):
    return lax.dot_general(a, b, (((1,), (1,)), ((), ())), preferred_element_type=F32)


def _in_proj_kernel(x_ref, g_ref, wq_ref, wkv_ref, wkr_ref, wx_ref, wg_ref, wa_ref, wb_ref,
                    cq_ref, ckv_ref, kr_ref, xl_ref, gl_ref, ga_ref, gb_ref):
    h = _rms_rows(x_ref[...], g_ref[...]).astype(BF16)
    cq_ref[...] = _dot(h, wq_ref[...])
    ckv_ref[...] = _dot(h, wkv_ref[...])
    kr_ref[...] = _dot(h, wkr_ref[...])
    _store_slabs(xl_ref, _dot(h, wx_ref[...]))
    _store_slabs(gl_ref, _dot(h, wg_ref[...]))
    ga_ref[...] = _dot(h, wa_ref[...]).astype(BF16)
    gb_ref[...] = _dot(h, wb_ref[...]).astype(BF16)


def _store_slabs(ref, v):
    for j in range(ref.shape[0]):
        ref[j] = v[:, j * LANES:(j + 1) * LANES].astype(ref.dtype)


def _load_slabs(ref):
    return jnp.concatenate([ref[j] for j in range(ref.shape[0])], axis=1)


def _in_proj(x2d, norm_g, w_parts):
    m = x2d.shape[0]
    row = lambda i: (i, 0)
    slab = lambda i: (0, i, 0)
    n_slab = LRU_WIDTH // LANES
    plain = lambda n, dt: (pl.BlockSpec((TM_PROJ, n), row), jax.ShapeDtypeStruct((m, n), dt))
    slabs = (pl.BlockSpec((n_slab, TM_PROJ, LANES), slab), jax.ShapeDtypeStruct((n_slab, m, LANES), F32))
    outs = [plain(Q_RANK, F32), plain(KV_RANK, F32), plain(LANES, F32), slabs, slabs,
            plain(D_MODEL, BF16), plain(D_MODEL, BF16)]
    return pl.pallas_call(
        _in_proj_kernel,
        grid=(m // TM_PROJ,),
        in_specs=[pl.BlockSpec((TM_PROJ, D_MODEL), row), _const_spec((1, D_MODEL))]
        + [_const_spec(w.shape) for w in w_parts],
        out_specs=[o[0] for o in outs],
        out_shape=[o[1] for o in outs],
        compiler_params=_params("parallel"),
        name="in_proj",
    )(x2d, norm_g, *w_parts)


def _mla_prep_kernel(cq_ref, ckv_ref, kr_ref, pos_ref, freq_ref, off_ref, qlg_ref, kvlg_ref,
                     wqt_ref, wkn_ref, wvt_ref, gqn_ref, gqr_ref, gkn_ref, gkr_ref,
                     qt_ref, kn_ref, krope_ref, vt_ref):
    tm = cq_ref.shape[0]
    pad_row = lax.broadcasted_iota(jnp.int32, (QK_PAD - QK_DIM, tm), 0)
    ang = freq_ref[...] * pos_ref[0].astype(F32)
    cos = jnp.cos(ang)
    sin = jnp.sin(ang)

    def rope_cols(v):
        x1, x2 = v[:HALF_ROPE], v[HALF_ROPE:]
        return x1 * cos - x2 * sin, x2 * cos + x1 * sin

    cqn = _rms_rows(cq_ref[...], qlg_ref[...]).astype(BF16)
    q_t = _dot_nt(wqt_ref[...], cqn)
    pad = jnp.where(pad_row == 0, off_ref[...], 0.0).astype(BF16)
    for h in range(HEADS):
        base = h * QK_DIM
        qn = _rms_cols(q_t[base:base + NOPE], gqn_ref[...])
        qr = _rms_cols(q_t[base + NOPE:base + QK_DIM], gqr_ref[...])
        o1, o2 = rope_cols(qr)
        q_pad = jnp.concatenate([qn.astype(BF16), o1.astype(BF16), o2.astype(BF16), pad], axis=0)
        for c in range(qt_ref.shape[2]):
            qt_ref[0, h, c] = q_pad[:, c * TQ:(c + 1) * TQ]

    ckvn = _rms_rows(ckv_ref[...], kvlg_ref[...]).astype(BF16)
    kn_all = _dot(ckvn, wkn_ref[...])
    v_t = _dot_nt(wvt_ref[...], ckvn)
    n_chunks = vt_ref.shape[2]
    for h in range(HEADS):
        kn = _rms_rows(kn_all[:, h * NOPE:(h + 1) * NOPE], gkn_ref[...])
        kn_ref[0, h] = kn.astype(BF16)
        for c in range(n_chunks):
            vt_ref[0, h, c] = v_t[h * V_DIM:(h + 1) * V_DIM, c * TK:(c + 1) * TK].astype(BF16)

    kr_t = kr_ref[...].T
    krn = _rms_cols(kr_t[:ROPE], gkr_ref[...])
    o1, o2 = rope_cols(krn)
    kr_out = jnp.concatenate([o1, o2, jnp.where(pad_row == 0, 1.0, 0.0)], axis=0)
    krope_ref[0] = kr_out.T.astype(BF16)


def _mla_prep(cq, ckv, kr, positions3, inv_freq, offset, q_lat_g, kv_lat_g, wq_t, wkn, wv_t,
              gqn, gqr, gkn, gkr, batch, seq):
    tm = TM_PREP
    nblk = seq // tm
    tok = lambda b, i: (b * nblk + i, 0)
    in_specs = [
        pl.BlockSpec((tm, Q_RANK), tok),
        pl.BlockSpec((tm, KV_RANK), tok),
        pl.BlockSpec((tm, LANES), tok),
        pl.BlockSpec((1, 1, tm), lambda b, i: (b, 0, i)),
        _const_spec(inv_freq.shape), _const_spec(offset.shape),
        _const_spec(q_lat_g.shape), _const_spec(kv_lat_g.shape),
        _const_spec(wq_t.shape), _const_spec(wkn.shape), _const_spec(wv_t.shape),
        _const_spec(gqn.shape), _const_spec(gqr.shape), _const_spec(gkn.shape), _const_spec(gkr.shape),
    ]
    out_specs = [
        pl.BlockSpec((1, HEADS, tm // TQ, QK_PAD, TQ), lambda b, i: (b, 0, i, 0, 0)),
        pl.BlockSpec((1, HEADS, tm, NOPE), lambda b, i: (b, 0, i, 0)),
        pl.BlockSpec((1, tm, LANES), lambda b, i: (b, i, 0)),
        pl.BlockSpec((1, HEADS, tm // TK, V_DIM, TK), lambda b, i: (b, 0, i, 0, 0)),
    ]
    out_shape = [
        jax.ShapeDtypeStruct((batch, HEADS, seq // TQ, QK_PAD, TQ), BF16),
        jax.ShapeDtypeStruct((batch, HEADS, seq, NOPE), BF16),
        jax.ShapeDtypeStruct((batch, seq, LANES), BF16),
        jax.ShapeDtypeStruct((batch, HEADS, seq // TK, V_DIM, TK), BF16),
    ]
    return pl.pallas_call(
        _mla_prep_kernel,
        grid=(batch, nblk),
        in_specs=in_specs,
        out_specs=out_specs,
        out_shape=out_shape,
        compiler_params=_params("parallel", "parallel"),
        name="mla_prep",
    )(cq, ckv, kr, positions3, inv_freq, offset, q_lat_g, kv_lat_g, wq_t, wkn, wv_t, gqn, gqr, gkn, gkr)


def _scores_t(qt_ref, kn_ref, kr_ref, i, j):
    start = pl.multiple_of(j * TK, TK)
    k = jnp.concatenate([kn_ref[0, 0, pl.ds(start, TK), :], kr_ref[0, pl.ds(start, TK), :]], axis=1)
    return _dot(k, qt_ref[0, 0, i])


def _causal_mask(s, i, j):
    key = j * TK + lax.broadcasted_iota(jnp.int32, s.shape, 0)
    qry = i * TQ + lax.broadcasted_iota(jnp.int32, s.shape, 1)
    return jnp.where(key <= qry, s, -jnp.inf)


def _attention_online_kernel(qt_ref, kn_ref, kr_ref, vt_ref, o_ref, m_ref, l_ref, acc_ref):
    i = pl.program_id(2)
    m_ref[...] = jnp.full(m_ref.shape, -jnp.inf, F32)
    l_ref[...] = jnp.zeros(l_ref.shape, F32)
    acc_ref[...] = jnp.zeros(acc_ref.shape, F32)

    def step(j, masked):
        s = _scores_t(qt_ref, kn_ref, kr_ref, 0, j)
        if masked:
            s = _causal_mask(s, i, j)
        m_old = m_ref[...]
        m_new = jnp.maximum(m_old, jnp.max(s, axis=0, keepdims=True))
        alpha = jnp.exp2(m_old - m_new)
        p = jnp.exp2(s - m_new)
        l_ref[...] = alpha * l_ref[...] + jnp.sum(p, axis=0, keepdims=True)
        acc_ref[...] = alpha * acc_ref[...] + _dot(vt_ref[0, 0, j], p.astype(BF16))
        m_ref[...] = m_new

    def body(j, carry):
        step(j, masked=False)
        return carry

    lax.fori_loop(0, i, body, 0)
    step(i, masked=True)
    out_t = acc_ref[...] / l_ref[...]
    o_ref[0] = out_t.T.astype(o_ref.dtype)


def _attention_shifted_kernel(itab_ref, jtab_ref, qt_ref, kn_ref, kr_ref, vt_ref, o_ref,
                              s0_ref, s1_ref, p0_ref, p1_ref, acc_ref, *, n_pairs, diag_per_trip):
    ones = jnp.ones((SUM_ROWS, TK), BF16)
    s_slots = (s0_ref, s1_ref)
    p_slots = (p0_ref, p1_ref)
    half = TK // 2
    assert TQ == TK and half % LANES == 0 and diag_per_trip >= 2

    def scores(t, s_ref, diagonal):
        if not diagonal:
            s_ref[...] = _scores_t(qt_ref, kn_ref, kr_ref, itab_ref[t], jtab_ref[t])
            return
        start = pl.multiple_of(jtab_ref[t] * TK, TK)
        k = jnp.concatenate([kn_ref[0, 0, pl.ds(start, TK), :], kr_ref[0, pl.ds(start, TK), :]], axis=1)
        q_t = qt_ref[0, 0, itab_ref[t]]
        s_ref[0:half, :] = _dot(k[0:half], q_t)
        s_ref[half:, half:] = _dot(k[half:], q_t[:, half:])

    def probs(t, s_ref, p_ref, diagonal):
        if not diagonal:
            p_ref[...] = jnp.exp2(s_ref[...]).astype(BF16)
            return

        def masked_exp2(s):
            key = lax.broadcasted_iota(jnp.int32, s.shape, 0)
            qry = lax.broadcasted_iota(jnp.int32, s.shape, 1)
            return jnp.exp2(jnp.where(key <= qry, s, -jnp.inf)).astype(BF16)

        p_ref[0:half, :] = masked_exp2(s_ref[0:half, :])
        p_ref[half:, half:] = masked_exp2(s_ref[half:, half:])

    def values(t, p_ref, diagonal):
        i = itab_ref[t]
        v_aug = jnp.concatenate([vt_ref[0, 0, jtab_ref[t]], ones], axis=0)
        if not diagonal:
            acc_ref[i] += _dot(v_aug, p_ref[...])
            return
        acc_ref[i, :, 0:half] += _dot(v_aug[:, 0:half], p_ref[0:half, 0:half])
        acc_ref[i, :, half:] += _dot(v_aug, p_ref[:, half:])

    def is_diagonal(position):
        return position % PAIRS_PER_TRIP < diag_per_trip

    acc_ref[...] = jnp.zeros(acc_ref.shape, F32)
    scores(0, s0_ref, diagonal=True)
    scores(1, s1_ref, diagonal=True)
    probs(0, s0_ref, p0_ref, diagonal=True)

    def body(u, carry):
        for sub in range(PAIRS_PER_TRIP):
            t = PAIRS_PER_TRIP * u + sub
            slot = sub % 2
            scores(t + 2, s_slots[slot], is_diagonal(sub + 2))
            probs(t + 1, s_slots[1 - slot], p_slots[1 - slot], is_diagonal(sub + 1))
            values(t, p_slots[slot], is_diagonal(sub))
        return carry

    lax.fori_loop(0, n_pairs // PAIRS_PER_TRIP, body, 0)

    for i in range(acc_ref.shape[0]):
        out_t = acc_ref[i, 0:V_DIM, :] / acc_ref[i, V_DIM:V_DIM + 1, :]
        o_ref[0, i * TQ:(i + 1) * TQ, :] = out_t.T.astype(o_ref.dtype)


def _attention_shifted(q_t, kn, krope, v_t, batch, seq):
    nq = seq // TQ
    diag = [(i, i) for i in range(nq)]
    below = [(i, j) for i in range(nq) for j in range(i)]
    n_pairs = len(diag) + len(below)
    trips = n_pairs // PAIRS_PER_TRIP
    assert TQ == TK and n_pairs % PAIRS_PER_TRIP == 0 and PAIRS_PER_TRIP % 2 == 0 and nq % trips == 0
    diag_per_trip = nq // trips
    below_per_trip = len(below) // trips
    pairs = []
    for u in range(trips):
        pairs += diag[u * diag_per_trip:(u + 1) * diag_per_trip]
        pairs += below[u * below_per_trip:(u + 1) * below_per_trip]
    pairs = pairs + [pairs[-1]] * 2
    itab = jnp.asarray([p[0] for p in pairs], jnp.int32)
    jtab = jnp.asarray([p[1] for p in pairs], jnp.int32)
    grid_spec = pltpu.PrefetchScalarGridSpec(
        num_scalar_prefetch=2,
        grid=(batch, HEADS),
        in_specs=[
            pl.BlockSpec((1, 1, nq, QK_PAD, TQ), lambda b, h, *_: (b, h, 0, 0, 0)),
            pl.BlockSpec((1, 1, seq, NOPE), lambda b, h, *_: (b, h, 0, 0)),
            pl.BlockSpec((1, seq, LANES), lambda b, h, *_: (b, 0, 0)),
            pl.BlockSpec((1, 1, seq // TK, V_DIM, TK), lambda b, h, *_: (b, h, 0, 0, 0)),
        ],
        out_specs=pl.BlockSpec((1, seq, V_DIM), lambda b, h, *_: (b, 0, h)),
        scratch_shapes=[
            pltpu.VMEM((TK, TQ), F32), pltpu.VMEM((TK, TQ), F32),
            pltpu.VMEM((TK, TQ), BF16), pltpu.VMEM((TK, TQ), BF16),
            pltpu.VMEM((nq, V_DIM + SUM_ROWS, TQ), F32),
        ],
    )
    return pl.pallas_call(
        functools.partial(_attention_shifted_kernel, n_pairs=n_pairs, diag_per_trip=diag_per_trip),
        grid_spec=grid_spec,
        out_shape=jax.ShapeDtypeStruct((batch, seq, HEADS * V_DIM), BF16),
        compiler_params=_params("parallel", "parallel"),
        name="attention_shifted",
    )(itab, jtab, q_t, kn, krope, v_t)


def _attention_online(q_t, kn, krope, v_t, batch, seq):
    assert TQ == TK
    nq = seq // TQ
    return pl.pallas_call(
        _attention_online_kernel,
        grid=(batch, HEADS, nq),
        in_specs=[
            pl.BlockSpec((1, 1, 1, QK_PAD, TQ), lambda b, h, i: (b, h, i, 0, 0)),
            pl.BlockSpec((1, 1, seq, NOPE), lambda b, h, i: (b, h, 0, 0)),
            pl.BlockSpec((1, seq, LANES), lambda b, h, i: (b, 0, 0)),
            pl.BlockSpec((1, 1, seq // TK, V_DIM, TK), lambda b, h, i: (b, h, 0, 0, 0)),
        ],
        out_specs=pl.BlockSpec((1, TQ, V_DIM), lambda b, h, i: (b, i, h)),
        out_shape=jax.ShapeDtypeStruct((batch, seq, HEADS * V_DIM), BF16),
        scratch_shapes=[pltpu.VMEM((1, TQ), F32), pltpu.VMEM((1, TQ), F32), pltpu.VMEM((V_DIM, TQ), F32)],
        compiler_params=_params("parallel", "parallel", "arbitrary"),
        name="attention_online",
    )(q_t, kn, krope, v_t)


def _rglru_kernel(x_ref, g_ref, cw_ref, cb_ref, wa_ref, ba_ref, wx_ref, bx_ref, lam_ref,
                  y_ref, a_ref, b_ref, tail_ref, h_ref):
    t = pl.program_id(1)

    @pl.when(t == 0)
    def _():
        tail_ref[...] = jnp.zeros(tail_ref.shape, F32)
        h_ref[...] = jnp.zeros(h_ref.shape, F32)

    def group(n, carry):
        _rglru_group(n, x_ref, g_ref, cw_ref, cb_ref, wa_ref, ba_ref, wx_ref, bx_ref, lam_ref,
                     y_ref, a_ref, b_ref, tail_ref, h_ref)
        return carry

    lax.fori_loop(0, LRU_WIDTH // MXU_DIM, group, 0, unroll=True)


def _rglru_group(n, x_ref, g_ref, cw_ref, cb_ref, wa_ref, ba_ref, wx_ref, bx_ref, lam_ref,
                 y_ref, a_ref, b_ref, tail_ref, h_ref):
    R, L, W = LRU_CHUNKS, LRU_CHUNK_LEN, MXU_DIM
    slabs = W // LANES
    chunk = lax.broadcasted_iota(jnp.int32, (R, W), 0)

    def chunk_rows(ref, tau):
        return jnp.concatenate(
            [ref[n * slabs + j, pl.ds(tau, R, stride=L), :] for j in range(slabs)], axis=1)

    def step_rows(v, tau):
        return v[tau * R:(tau + 1) * R]

    def from_prev_chunk(v, first):
        return jnp.where(chunk == 0, first, pltpu.roll(v, 1, 0))

    x = jnp.concatenate([chunk_rows(x_ref, tau) for tau in range(L)], axis=0)
    tails = tail_ref[n]
    lead = [from_prev_chunk(step_rows(x, L - m), tails[m - 1:m, :]) for m in range(CONV_WIDTH - 1, 0, -1)]
    ext = jnp.concatenate(lead + [x], axis=0)
    tail_ref[n] = jnp.concatenate(
        [x[(L - m + 1) * R - 1:(L - m + 1) * R] for m in range(1, CONV_WIDTH)], axis=0)

    cw = cw_ref[n]
    xc = cb_ref[n] + ext[0:L * R] * cw[0:1, :]
    for tap in range(1, CONV_WIDTH):
        xc = xc + ext[tap * R:(tap + L) * R] * cw[tap:tap + 1, :]

    xcb = xc.astype(BF16)
    r = 1.0 / (1.0 + jnp.exp2(_dot(xcb, wa_ref[n]) + ba_ref[n]))
    gi = 1.0 / (1.0 + jnp.exp2(_dot(xcb, wx_ref[n]) + bx_ref[n]))

    lam = lam_ref[n]
    softplus_neg = jnp.maximum(-lam, 0.0) + jnp.log1p(jnp.exp(-jnp.abs(lam)))
    neg_log_a = r * (LRU_C * softplus_neg)
    a = jnp.exp2(neg_log_a * (-LOG2_E))
    m2 = jnp.tanh(neg_log_a) * (1.0 + a * a)
    mult = m2 * lax.rsqrt(jnp.maximum(m2, SQRT_FLOOR))
    a_ref[...] = a
    b_ref[...] = mult * (gi * xc)

    a_cum = a_ref[0:R, :]
    h_loc = b_ref[0:R, :]
    for tau in range(1, L):
        a_t = a_ref[tau * R:(tau + 1) * R, :]
        h_loc = a_t * h_loc + b_ref[tau * R:(tau + 1) * R, :]
        a_cum = a_t * a_cum
        a_ref[tau * R:(tau + 1) * R, :] = a_cum
        b_ref[tau * R:(tau + 1) * R, :] = h_loc

    carry = h_ref[n]
    sh = 1
    while sh < R:
        keep = chunk >= sh
        a_prev = jnp.where(keep, pltpu.roll(a_cum, sh, 0), 1.0)
        h_prev = jnp.where(keep, pltpu.roll(h_loc, sh, 0), 0.0)
        h_loc = a_cum * h_prev + h_loc
        a_cum = a_cum * a_prev
        sh *= 2
    h_end = a_cum * carry + h_loc
    h_in = from_prev_chunk(h_end, carry)
    h_ref[n] = h_end[R - 1:R, :]

    c0 = math.sqrt(2.0 / math.pi)
    for tau in range(L):
        h = b_ref[tau * R:(tau + 1) * R, :] + a_ref[tau * R:(tau + 1) * R, :] * h_in
        g = chunk_rows(g_ref, tau)
        half_g = 0.5 * g
        gelu = half_g + half_g * jnp.tanh(g * (c0 + (c0 * 0.044715) * (g * g)))
        y = h * gelu
        for j in range(slabs):
            y_ref[n * slabs + j, pl.ds(tau, R, stride=L), :] = y[:, j * LANES:(j + 1) * LANES]


def _rglru(x_lru, g_lru, conv_w, conv_b, wa_blk, ba, wx_blk, bx, lam, batch, seq):
    ts = LRU_CHUNKS * LRU_CHUNK_LEN
    nblk = seq // ts
    n_slab = LRU_WIDTH // LANES
    n_grp = LRU_WIDTH // MXU_DIM
    per_group = lambda v: v.reshape(-1, n_grp, MXU_DIM).transpose(1, 0, 2)
    conv_w, conv_b, ba, bx, lam = (per_group(v) for v in (conv_w, conv_b, ba, bx, lam))
    tile = pl.BlockSpec((n_slab, ts, LANES), lambda b, t: (0, b * nblk + t, 0))
    return pl.pallas_call(
        _rglru_kernel,
        grid=(batch, nblk),
        in_specs=[
            tile, tile,
            _const_spec(conv_w.shape), _const_spec(conv_b.shape),
            _const_spec(wa_blk.shape), _const_spec(ba.shape),
            _const_spec(wx_blk.shape), _const_spec(bx.shape), _const_spec(lam.shape),
        ],
        out_specs=tile,
        out_shape=jax.ShapeDtypeStruct((n_slab, batch * seq, LANES), F32),
        scratch_shapes=[
            pltpu.VMEM((ts, MXU_DIM), F32),
            pltpu.VMEM((ts, MXU_DIM), F32),
            pltpu.VMEM((n_grp, CONV_WIDTH - 1, MXU_DIM), F32),
            pltpu.VMEM((n_grp, 1, MXU_DIM), F32),
        ],
        compiler_params=_params("parallel", "arbitrary"),
        name="rglru",
    )(x_lru, g_lru, conv_w, conv_b, wa_blk, ba, wx_blk, bx, lam)


def _merge_kernel(x_ref, ya_ref, yb_ref, ga_ref, gb_ref, wpa_ref, wpl_ref, wo_ref, o_ref):
    pa = _dot(ya_ref[...], wpa_ref[...])
    pb = _dot(_load_slabs(yb_ref).astype(BF16), wpl_ref[...])
    merged = _sigmoid(ga_ref[...].astype(F32)) * pa + _sigmoid(gb_ref[...].astype(F32)) * pb
    o_ref[...] = x_ref[...] + _dot(merged.astype(BF16), wo_ref[...])


def _merge(x2d, ya, yb, ga, gb, wpa, wpl, wo):
    m = x2d.shape[0]
    row = lambda i: (i, 0)
    tile = pl.BlockSpec((TM_PROJ, D_MODEL), row)
    slabs = pl.BlockSpec((yb.shape[0], TM_PROJ, LANES), lambda i: (0, i, 0))
    return pl.pallas_call(
        _merge_kernel,
        grid=(m // TM_PROJ,),
        in_specs=[tile, tile, slabs, tile, tile]
        + [_const_spec(wpa.shape), _const_spec(wpl.shape), _const_spec(wo.shape)],
        out_specs=tile,
        out_shape=jax.ShapeDtypeStruct((m, D_MODEL), F32),
        compiler_params=_params("parallel"),
        name="merge",
    )(x2d, ya, yb, ga, gb, wpa, wpl, wo)


def _ffn_kernel(x_ref, g_ref, wg_ref, wu_ref, wd_ref, o_ref):
    x = x_ref[...]
    h = _rms_rows(x, g_ref[...]).astype(BF16)
    gate = _dot(h, wg_ref[...])
    up = _dot(h, wu_ref[...])
    act = (gate * _sigmoid(gate) * up).astype(BF16)
    o_ref[...] = x + _dot(act, wd_ref[...])


def _ffn(x2d, norm_g, wg, wu, wd):
    m = x2d.shape[0]
    tm = TM_FFN
    row = lambda i: (i, 0)
    tile = pl.BlockSpec((tm, D_MODEL), row)
    return pl.pallas_call(
        _ffn_kernel,
        grid=(m // tm,),
        in_specs=[tile, _const_spec(norm_g.shape), _const_spec(wg.shape), _const_spec(wu.shape),
                  _const_spec(wd.shape)],
        out_specs=tile,
        out_shape=jax.ShapeDtypeStruct((m, D_MODEL), F32),
        compiler_params=_params("parallel"),
        name="ffn",
    )(x2d, norm_g, wg, wu, wd)


def _block_diag_groups(w):
    per = MXU_DIM // LRU_BLOCK
    n_grp = w.shape[0] // per
    w4 = w.reshape(n_grp, per, LRU_BLOCK, LRU_BLOCK)
    eye = jnp.eye(per, dtype=w.dtype)
    out = jnp.einsum("gpde,pq->gpdqe", w4, eye)
    return out.reshape(n_grp, MXU_DIM, MXU_DIM)


def kernel(x, positions, norm_mix_g, w_in, q_lat_g, w_q_up, kv_lat_g, w_kv_up, q_head_g, k_head_g,
           conv_w, conv_b, lru_wa, lru_ba, lru_wx, lru_bx, lru_lambda, w_proj_attn, w_proj_lru,
           w_out, norm_ffn_g, w_ffn_gate, w_ffn_up, w_ffn_down):
    batch, seq, d = x.shape
    depth = w_in.shape[0]
    half = HALF_ROPE
    inv_freq = (ROPE_THETA ** (-jnp.arange(half, dtype=F32) / half)).reshape(half, 1)
    positions3 = positions.reshape(batch, 1, seq)
    x2d = x.reshape(batch * seq, d)

    for l in range(depth):
        w = w_in[l].astype(BF16)
        c0 = Q_RANK
        c1 = c0 + KV_RANK
        c2 = c1 + ROPE
        c3 = c2 + LRU_WIDTH
        c4 = c3 + LRU_WIDTH
        c5 = c4 + D_MODEL
        w_kr = jnp.pad(w[:, c1:c2], ((0, 0), (0, LANES - ROPE)))
        w_parts = [w[:, :c0], w[:, c0:c1], w_kr, w[:, c2:c3], w[:, c3:c4], w[:, c4:c5], w[:, c5:]]
        cq, ckv, kr, x_lru, g_lru, gate_a, gate_b = _in_proj(x2d, norm_mix_g[l].reshape(1, d), w_parts)

        wkv = w_kv_up[l].astype(BF16).reshape(KV_RANK, HEADS, NOPE + V_DIM)
        wkn = wkv[:, :, :NOPE].reshape(KV_RANK, HEADS * NOPE)
        wv_t = wkv[:, :, NOPE:].reshape(KV_RANK, HEADS * V_DIM).T
        wq_t = w_q_up[l].astype(BF16).T
        amax = lambda v: jnp.max(jnp.abs(v))
        gq, gk = q_head_g[l], k_head_g[l]
        bound2 = Q_SCALE * (NOPE * amax(gq[:NOPE]) * amax(gk[:NOPE]) + ROPE * amax(gq[NOPE:]) * amax(gk[NOPE:]))
        shift_ok = bound2 <= SHIFT_LIMIT_LOG2
        offset = jnp.where(shift_ok, -bound2, 0.0).reshape(1, 1)
        q_t, kn, krope, v_t = _mla_prep(
            cq, ckv, kr, positions3, inv_freq, offset,
            q_lat_g[l].reshape(1, Q_RANK), kv_lat_g[l].reshape(1, KV_RANK), wq_t, wkn, wv_t,
            (gq[:NOPE] * Q_SCALE).reshape(NOPE, 1), (gq[NOPE:] * Q_SCALE).reshape(ROPE, 1),
            k_head_g[l][:NOPE].reshape(1, NOPE), k_head_g[l][NOPE:].reshape(ROPE, 1),
            batch, seq)
        y_a = lax.cond(
            shift_ok,
            functools.partial(_attention_shifted, batch=batch, seq=seq),
            functools.partial(_attention_online, batch=batch, seq=seq),
            q_t, kn, krope, v_t).reshape(batch * seq, HEADS * V_DIM)

        y_b = _rglru(
            x_lru, g_lru, conv_w[l], conv_b[l].reshape(1, LRU_WIDTH),
            _block_diag_groups(lru_wa[l] * -LOG2_E).astype(BF16), (lru_ba[l] * -LOG2_E).reshape(1, LRU_WIDTH),
            _block_diag_groups(lru_wx[l] * -LOG2_E).astype(BF16), (lru_bx[l] * -LOG2_E).reshape(1, LRU_WIDTH),
            lru_lambda[l].reshape(1, LRU_WIDTH), batch, seq)

        x2d = _merge(x2d, y_a, y_b, gate_a, gate_b, w_proj_attn[l].astype(BF16),
                     w_proj_lru[l].astype(BF16), w_out[l].astype(BF16))
        x2d = _ffn(x2d, norm_ffn_g[l].reshape(1, d), w_ffn_gate[l].astype(BF16),
                   w_ffn_up[l].astype(BF16), w_ffn_down[l].astype(BF16))
    return x2d.reshape(batch, seq, d)
```

```python
import functools
import math

import jax
import jax.numpy as jnp
from jax import lax
from jax.experimental import pallas as pl
from jax.experimental.pallas import tpu as pltpu

D_MODEL = 1024
HEADS = 8
NOPE = 128
ROPE = 64
HALF_ROPE = ROPE // 2
QK_DIM = NOPE + ROPE
V_DIM = 128
Q_RANK = 256
KV_RANK = 256
ROPE_THETA = 10000.0
LRU_WIDTH = 1024
LRU_BLOCK = 64
CONV_WIDTH = 4
LRU_C = 8.0
EPS = 1e-6

LANES = 128
SUBLANES = 8
MXU_DIM = 256
QK_PAD = 2 * LANES
SUM_ROWS = 16

LOG2_E = math.log2(math.e)
Q_SCALE = QK_DIM ** -0.5 * LOG2_E
SHIFT_LIMIT_LOG2 = 60.0

VMEM_LIMIT = 56 * 1024 * 1024

TM_PROJ = 1024
TM_FFN = 512
TM_PREP = 2048
TQ = 512
TK = 512
LRU_CHUNKS = 16
LRU_CHUNK_LEN = 16
LRU_BLOCKS_PER_STEP = 2
SQRT_FLOOR = 1e-30
PAIRS_PER_TRIP = 34

F32 = jnp.float32
BF16 = jnp.bfloat16


def _params(*semantics):
    return pltpu.CompilerParams(dimension_semantics=semantics, vmem_limit_bytes=VMEM_LIMIT)


def _const_spec(shape):
    zeros = (0,) * len(shape)
    return pl.BlockSpec(shape, lambda *_: zeros, pipeline_mode=pl.Buffered(1))


def _sigmoid(v):
    return 1.0 / (1.0 + jnp.exp(-v))


def _rms_rows(v, gain_row):
    ms = jnp.mean(v * v, axis=-1, keepdims=True)
    return v * lax.rsqrt(ms + EPS) * gain_row


def _rms_cols(v, gain_col):
    ms = jnp.mean(v * v, axis=0, keepdims=True)
    return v * lax.rsqrt(ms + EPS) * gain_col


def _dot(a, b):
    return jnp.dot(a, b, preferred_element_type=F32)


def _dot_nt(a, b):
    return lax.dot_general(a, b, (((1,), (1,)), ((), ())), preferred_element_type=F32)


def _in_proj_kernel(x_ref, g_ref, wq_ref, wkv_ref, wkr_ref, wx_ref, wg_ref, wa_ref, wb_ref,
                    cq_ref, ckv_ref, kr_ref, xl_ref, gl_ref, ga_ref, gb_ref):
    h = _rms_rows(x_ref[...], g_ref[...]).astype(BF16)
    cq_ref[...] = _dot(h, wq_ref[...])
    ckv_ref[...] = _dot(h, wkv_ref[...])
    kr_ref[...] = _dot(h, wkr_ref[...])
    _store_slabs(xl_ref, _dot(h, wx_ref[...]))
    _store_slabs(gl_ref, _dot(h, wg_ref[...]))
    ga_ref[...] = _dot(h, wa_ref[...]).astype(BF16)
    gb_ref[...] = _dot(h, wb_ref[...]).astype(BF16)


def _store_slabs(ref, v):
    for j in range(ref.shape[0]):
        ref[j] = v[:, j * LANES:(j + 1) * LANES].astype(ref.dtype)


def _load_slabs(ref):
    return jnp.concatenate([ref[j] for j in range(ref.shape[0])], axis=1)


def _in_proj(x2d, norm_g, w_parts):
    m = x2d.shape[0]
    row = lambda i: (i, 0)
    slab = lambda i: (0, i, 0)
    n_slab = LRU_WIDTH // LANES
    plain = lambda n, dt: (pl.BlockSpec((TM_PROJ, n), row), jax.ShapeDtypeStruct((m, n), dt))
    slabs = (pl.BlockSpec((n_slab, TM_PROJ, LANES), slab), jax.ShapeDtypeStruct((n_slab, m, LANES), F32))
    outs = [plain(Q_RANK, F32), plain(KV_RANK, F32), plain(LANES, F32), slabs, slabs,
            plain(D_MODEL, BF16), plain(D_MODEL, BF16)]
    return pl.pallas_call(
        _in_proj_kernel,
        grid=(m // TM_PROJ,),
        in_specs=[pl.BlockSpec((TM_PROJ, D_MODEL), row), _const_spec((1, D_MODEL))]
        + [_const_spec(w.shape) for w in w_parts],
        out_specs=[o[0] for o in outs],
        out_shape=[o[1] for o in outs],
        compiler_params=_params("parallel"),
        name="in_proj",
    )(x2d, norm_g, *w_parts)


def _mla_prep_kernel(cq_ref, ckv_ref, kr_ref, pos_ref, freq_ref, off_ref, qlg_ref, kvlg_ref,
                     wqt_ref, wkn_ref, wvt_ref, gqn_ref, gqr_ref, gkn_ref, gkr_ref,
                     qt_ref, kn_ref, krope_ref, vt_ref):
    tm = cq_ref.shape[0]
    pad_row = lax.broadcasted_iota(jnp.int32, (QK_PAD - QK_DIM, tm), 0)
    ang = freq_ref[...] * pos_ref[0].astype(F32)
    cos = jnp.cos(ang)
    sin = jnp.sin(ang)

    def rope_cols(v):
        x1, x2 = v[:HALF_ROPE], v[HALF_ROPE:]
        return x1 * cos - x2 * sin, x2 * cos + x1 * sin

    cqn = _rms_rows(cq_ref[...], qlg_ref[...]).astype(BF16)
    q_t = _dot_nt(wqt_ref[...], cqn)
    pad = jnp.where(pad_row == 0, off_ref[...], 0.0).astype(BF16)
    for h in range(HEADS):
        base = h * QK_DIM
        qn = _rms_cols(q_t[base:base + NOPE], gqn_ref[...])
        qr = _rms_cols(q_t[base + NOPE:base + QK_DIM], gqr_ref[...])
        o1, o2 = rope_cols(qr)
        q_pad = jnp.concatenate([qn.astype(BF16), o1.astype(BF16), o2.astype(BF16), pad], axis=0)
        for c in range(qt_ref.shape[2]):
            qt_ref[0, h, c] = q_pad[:, c * TQ:(c + 1) * TQ]

    ckvn = _rms_rows(ckv_ref[...], kvlg_ref[...]).astype(BF16)
    kn_all = _dot(ckvn, wkn_ref[...])
    v_t = _dot_nt(wvt_ref[...], ckvn)
    n_chunks = vt_ref.shape[2]
    for h in range(HEADS):
        kn = _rms_rows(kn_all[:, h * NOPE:(h + 1) * NOPE], gkn_ref[...])
        kn_ref[0, h] = kn.astype(BF16)
        for c in range(n_chunks):
            vt_ref[0, h, c] = v_t[h * V_DIM:(h + 1) * V_DIM, c * TK:(c + 1) * TK].astype(BF16)

    kr_t = kr_ref[...].T
    krn = _rms_cols(kr_t[:ROPE], gkr_ref[...])
    o1, o2 = rope_cols(krn)
    kr_out = jnp.concatenate([o1, o2, jnp.where(pad_row == 0, 1.0, 0.0)], axis=0)
    krope_ref[0] = kr_out.T.astype(BF16)


def _mla_prep(cq, ckv, kr, positions3, inv_freq, offset, q_lat_g, kv_lat_g, wq_t, wkn, wv_t,
              gqn, gqr, gkn, gkr, batch, seq):
    tm = TM_PREP
    nblk = seq // tm
    tok = lambda b, i: (b * nblk + i, 0)
    in_specs = [
        pl.BlockSpec((tm, Q_RANK), tok),
        pl.BlockSpec((tm, KV_RANK), tok),
        pl.BlockSpec((tm, LANES), tok),
        pl.BlockSpec((1, 1, tm), lambda b, i: (b, 0, i)),
        _const_spec(inv_freq.shape), _const_spec(offset.shape),
        _const_spec(q_lat_g.shape), _const_spec(kv_lat_g.shape),
        _const_spec(wq_t.shape), _const_spec(wkn.shape), _const_spec(wv_t.shape),
        _const_spec(gqn.shape), _const_spec(gqr.shape), _const_spec(gkn.shape), _const_spec(gkr.shape),
    ]
    out_specs = [
        pl.BlockSpec((1, HEADS, tm // TQ, QK_PAD, TQ), lambda b, i: (b, 0, i, 0, 0)),
        pl.BlockSpec((1, HEADS, tm, NOPE), lambda b, i: (b, 0, i, 0)),
        pl.BlockSpec((1, tm, LANES), lambda b, i: (b, i, 0)),
        pl.BlockSpec((1, HEADS, tm // TK, V_DIM, TK), lambda b, i: (b, 0, i, 0, 0)),
    ]
    out_shape = [
        jax.ShapeDtypeStruct((batch, HEADS, seq // TQ, QK_PAD, TQ), BF16),
        jax.ShapeDtypeStruct((batch, HEADS, seq, NOPE), BF16),
        jax.ShapeDtypeStruct((batch, seq, LANES), BF16),
        jax.ShapeDtypeStruct((batch, HEADS, seq // TK, V_DIM, TK), BF16),
    ]
    return pl.pallas_call(
        _mla_prep_kernel,
        grid=(batch, nblk),
        in_specs=in_specs,
        out_specs=out_specs,
        out_shape=out_shape,
        compiler_params=_params("parallel", "parallel"),
        name="mla_prep",
    )(cq, ckv, kr, positions3, inv_freq, offset, q_lat_g, kv_lat_g, wq_t, wkn, wv_t, gqn, gqr, gkn, gkr)


def _scores_t(qt_ref, kn_ref, kr_ref, i, j):
    start = pl.multiple_of(j * TK, TK)
    k = jnp.concatenate([kn_ref[0, 0, pl.ds(start, TK), :], kr_ref[0, pl.ds(start, TK), :]], axis=1)
    return _dot(k, qt_ref[0, 0, i])


def _causal_mask(s, i, j):
    key = j * TK + lax.broadcasted_iota(jnp.int32, s.shape, 0)
    qry = i * TQ + lax.broadcasted_iota(jnp.int32, s.shape, 1)
    return jnp.where(key <= qry, s, -jnp.inf)


def _attention_online_kernel(qt_ref, kn_ref, kr_ref, vt_ref, o_ref, m_ref, l_ref, acc_ref):
    i = pl.program_id(2)
    m_ref[...] = jnp.full(m_ref.shape, -jnp.inf, F32)
    l_ref[...] = jnp.zeros(l_ref.shape, F32)
    acc_ref[...] = jnp.zeros(acc_ref.shape, F32)

    def step(j, masked):
        s = _scores_t(qt_ref, kn_ref, kr_ref, 0, j)
        if masked:
            s = _causal_mask(s, i, j)
        m_old = m_ref[...]
        m_new = jnp.maximum(m_old, jnp.max(s, axis=0, keepdims=True))
        alpha = jnp.exp2(m_old - m_new)
        p = jnp.exp2(s - m_new)
        l_ref[...] = alpha * l_ref[...] + jnp.sum(p, axis=0, keepdims=True)
        acc_ref[...] = alpha * acc_ref[...] + _dot(vt_ref[0, 0, j], p.astype(BF16))
        m_ref[...] = m_new

    def body(j, carry):
        step(j, masked=False)
        return carry

    lax.fori_loop(0, i, body, 0)
    step(i, masked=True)
    out_t = acc_ref[...] / l_ref[...]
    o_ref[0] = out_t.T.astype(o_ref.dtype)


def _attention_shifted_kernel(itab_ref, jtab_ref, qt_ref, kn_ref, kr_ref, vt_ref, o_ref,
                              s0_ref, s1_ref, p0_ref, p1_ref, acc_ref, *, n_pairs, diag_per_trip):
    ones = jnp.ones((SUM_ROWS, TK), BF16)
    s_slots = (s0_ref, s1_ref)
    p_slots = (p0_ref, p1_ref)
    half = TK // 2
    assert TQ == TK and half % LANES == 0 and diag_per_trip >= 2

    def scores(t, s_ref, diagonal):
        if not diagonal:
            s_ref[...] = _scores_t(qt_ref, kn_ref, kr_ref, itab_ref[t], jtab_ref[t])
            return
        start = pl.multiple_of(jtab_ref[t] * TK, TK)
        k = jnp.concatenate([kn_ref[0, 0, pl.ds(start, TK), :], kr_ref[0, pl.ds(start, TK), :]], axis=1)
        q_t = qt_ref[0, 0, itab_ref[t]]
        s_ref[0:half, :] = _dot(k[0:half], q_t)
        s_ref[half:, half:] = _dot(k[half:], q_t[:, half:])

    def probs(t, s_ref, p_ref, diagonal):
        if not diagonal:
            p_ref[...] = jnp.exp2(s_ref[...]).astype(BF16)
            return

        def masked_exp2(s):
            key = lax.broadcasted_iota(jnp.int32, s.shape, 0)
            qry = lax.broadcasted_iota(jnp.int32, s.shape, 1)
            return jnp.exp2(jnp.where(key <= qry, s, -jnp.inf)).astype(BF16)

        p_ref[0:half, :] = masked_exp2(s_ref[0:half, :])
        p_ref[half:, half:] = masked_exp2(s_ref[half:, half:])

    def values(t, p_ref, diagonal):
        i = itab_ref[t]
        v_aug = jnp.concatenate([vt_ref[0, 0, jtab_ref[t]], ones], axis=0)
        if not diagonal:
            acc_ref[i] += _dot(v_aug, p_ref[...])
            return
        acc_ref[i, :, 0:half] += _dot(v_aug[:, 0:half], p_ref[0:half, 0:half])
        acc_ref[i, :, half:] += _dot(v_aug, p_ref[:, half:])

    def is_diagonal(position):
        return position % PAIRS_PER_TRIP < diag_per_trip

    acc_ref[...] = jnp.zeros(acc_ref.shape, F32)
    scores(0, s0_ref, diagonal=True)
    scores(1, s1_ref, diagonal=True)
    probs(0, s0_ref, p0_ref, diagonal=True)

    def body(u, carry):
        for sub in range(PAIRS_PER_TRIP):
            t = PAIRS_PER_TRIP * u + sub
            slot = sub % 2
            scores(t + 2, s_slots[slot], is_diagonal(sub + 2))
            probs(t + 1, s_slots[1 - slot], p_slots[1 - slot], is_diagonal(sub + 1))
            values(t, p_slots[slot], is_diagonal(sub))
        return carry

    lax.fori_loop(0, n_pairs // PAIRS_PER_TRIP, body, 0)

    for i in range(acc_ref.shape[0]):
        out_t = acc_ref[i, 0:V_DIM, :] / acc_ref[i, V_DIM:V_DIM + 1, :]
        o_ref[0, i * TQ:(i + 1) * TQ, :] = out_t.T.astype(o_ref.dtype)


def _attention_shifted(q_t, kn, krope, v_t, batch, seq):
    nq = seq // TQ
    diag = [(i, i) for i in range(nq)]
    below = [(i, j) for i in range(nq) for j in range(i)]
    n_pairs = len(diag) + len(below)
    trips = n_pairs // PAIRS_PER_TRIP
    assert TQ == TK and n_pairs % PAIRS_PER_TRIP == 0 and PAIRS_PER_TRIP % 2 == 0 and nq % trips == 0
    diag_per_trip = nq // trips
    below_per_trip = len(below) // trips
    pairs = []
    for u in range(trips):
        pairs += diag[u * diag_per_trip:(u + 1) * diag_per_trip]
        pairs += below[u * below_per_trip:(u + 1) * below_per_trip]
    pairs = pairs + [pairs[-1]] * 2
    itab = jnp.asarray([p[0] for p in pairs], jnp.int32)
    jtab = jnp.asarray([p[1] for p in pairs], jnp.int32)
    grid_spec = pltpu.PrefetchScalarGridSpec(
        num_scalar_prefetch=2,
        grid=(batch, HEADS),
        in_specs=[
            pl.BlockSpec((1, 1, nq, QK_PAD, TQ), lambda b, h, *_: (b, h, 0, 0, 0)),
            pl.BlockSpec((1, 1, seq, NOPE), lambda b, h, *_: (b, h, 0, 0)),
            pl.BlockSpec((1, seq, LANES), lambda b, h, *_: (b, 0, 0)),
            pl.BlockSpec((1, 1, seq // TK, V_DIM, TK), lambda b, h, *_: (b, h, 0, 0, 0)),
        ],
        out_specs=pl.BlockSpec((1, seq, V_DIM), lambda b, h, *_: (b, 0, h)),
        scratch_shapes=[
            pltpu.VMEM((TK, TQ), F32), pltpu.VMEM((TK, TQ), F32),
            pltpu.VMEM((TK, TQ), BF16), pltpu.VMEM((TK, TQ), BF16),
            pltpu.VMEM((nq, V_DIM + SUM_ROWS, TQ), F32),
        ],
    )
    return pl.pallas_call(
        functools.partial(_attention_shifted_kernel, n_pairs=n_pairs, diag_per_trip=diag_per_trip),
        grid_spec=grid_spec,
        out_shape=jax.ShapeDtypeStruct((batch, seq, HEADS * V_DIM), BF16),
        compiler_params=_params("parallel", "parallel"),
        name="attention_shifted",
    )(itab, jtab, q_t, kn, krope, v_t)


def _attention_online(q_t, kn, krope, v_t, batch, seq):
    assert TQ == TK
    nq = seq // TQ
    return pl.pallas_call(
        _attention_online_kernel,
        grid=(batch, HEADS, nq),
        in_specs=[
            pl.BlockSpec((1, 1, 1, QK_PAD, TQ), lambda b, h, i: (b, h, i, 0, 0)),
            pl.BlockSpec((1, 1, seq, NOPE), lambda b, h, i: (b, h, 0, 0)),
            pl.BlockSpec((1, seq, LANES), lambda b, h, i: (b, 0, 0)),
            pl.BlockSpec((1, 1, seq // TK, V_DIM, TK), lambda b, h, i: (b, h, 0, 0, 0)),
        ],
        out_specs=pl.BlockSpec((1, TQ, V_DIM), lambda b, h, i: (b, i, h)),
        out_shape=jax.ShapeDtypeStruct((batch, seq, HEADS * V_DIM), BF16),
        scratch_shapes=[pltpu.VMEM((1, TQ), F32), pltpu.VMEM((1, TQ), F32), pltpu.VMEM((V_DIM, TQ), F32)],
        compiler_params=_params("parallel", "parallel", "arbitrary"),
        name="attention_online",
    )(q_t, kn, krope, v_t)


def _rglru_kernel(x_ref, g_ref, cw_ref, cb_ref, wa_ref, ba_ref, wx_ref, bx_ref, lam_ref,
                  y_ref, a_ref, b_ref, tail_ref, h_ref):
    t = pl.program_id(1)

    @pl.when(t == 0)
    def _():
        tail_ref[...] = jnp.zeros(tail_ref.shape, F32)
        h_ref[...] = jnp.zeros(h_ref.shape, F32)

    block_rows = LRU_CHUNKS * LRU_CHUNK_LEN
    for row0 in range(0, x_ref.shape[1], block_rows):
        for n in range(LRU_WIDTH // MXU_DIM):
            _rglru_group(n, row0, x_ref, g_ref, cw_ref, cb_ref, wa_ref, ba_ref, wx_ref, bx_ref, lam_ref,
                         y_ref, a_ref, b_ref, tail_ref, h_ref)


def _rglru_group(n, row0, x_ref, g_ref, cw_ref, cb_ref, wa_ref, ba_ref, wx_ref, bx_ref, lam_ref,
                 y_ref, a_ref, b_ref, tail_ref, h_ref):
    R, L, W = LRU_CHUNKS, LRU_CHUNK_LEN, MXU_DIM
    slabs = W // LANES
    chunk = lax.broadcasted_iota(jnp.int32, (R, W), 0)

    def chunk_rows(ref, tau):
        return jnp.concatenate(
            [ref[n * slabs + j, pl.ds(row0 + tau, R, stride=L), :] for j in range(slabs)], axis=1)

    def step_rows(v, tau):
        return v[tau * R:(tau + 1) * R]

    def from_prev_chunk(v, first):
        return jnp.where(chunk == 0, first, pltpu.roll(v, 1, 0))

    x = jnp.concatenate([chunk_rows(x_ref, tau) for tau in range(L)], axis=0)
    tails = tail_ref[n]
    lead = [from_prev_chunk(step_rows(x, L - m), tails[m - 1:m, :]) for m in range(CONV_WIDTH - 1, 0, -1)]
    ext = jnp.concatenate(lead + [x], axis=0)
    tail_ref[n] = jnp.concatenate(
        [x[(L - m + 1) * R - 1:(L - m + 1) * R] for m in range(1, CONV_WIDTH)], axis=0)

    cw = cw_ref[n]
    xc = cb_ref[n] + ext[0:L * R] * cw[0:1, :]
    for tap in range(1, CONV_WIDTH):
        xc = xc + ext[tap * R:(tap + L) * R] * cw[tap:tap + 1, :]

    xcb = xc.astype(BF16)
    r = 1.0 / (1.0 + jnp.exp2(_dot(xcb, wa_ref[n]) + ba_ref[n]))
    gi = 1.0 / (1.0 + jnp.exp2(_dot(xcb, wx_ref[n]) + bx_ref[n]))

    lam = lam_ref[n]
    softplus_neg = jnp.maximum(-lam, 0.0) + jnp.log1p(jnp.exp(-jnp.abs(lam)))
    neg_log_a = r * (LRU_C * softplus_neg)
    a = jnp.exp2(neg_log_a * (-LOG2_E))
    m2 = jnp.tanh(neg_log_a) * (1.0 + a * a)
    mult = m2 * lax.rsqrt(jnp.maximum(m2, SQRT_FLOOR))
    a_ref[...] = a
    b_ref[...] = mult * (gi * xc)

    a_cum = a_ref[0:R, :]
    h_loc = b_ref[0:R, :]
    for tau in range(1, L):
        a_t = a_ref[tau * R:(tau + 1) * R, :]
        h_loc = a_t * h_loc + b_ref[tau * R:(tau + 1) * R, :]
        a_cum = a_t * a_cum
        a_ref[tau * R:(tau + 1) * R, :] = a_cum
        b_ref[tau * R:(tau + 1) * R, :] = h_loc

    carry = h_ref[n]
    sh = 1
    while sh < R:
        keep = chunk >= sh
        a_prev = jnp.where(keep, pltpu.roll(a_cum, sh, 0), 1.0)
        h_prev = jnp.where(keep, pltpu.roll(h_loc, sh, 0), 0.0)
        h_loc = a_cum * h_prev + h_loc
        a_cum = a_cum * a_prev
        sh *= 2
    h_end = a_cum * carry + h_loc
    h_in = from_prev_chunk(h_end, carry)
    h_ref[n] = h_end[R - 1:R, :]

    c0 = math.sqrt(2.0 / math.pi)
    for tau in range(L):
        h = b_ref[tau * R:(tau + 1) * R, :] + a_ref[tau * R:(tau + 1) * R, :] * h_in
        g = chunk_rows(g_ref, tau)
        half_g = 0.5 * g
        gelu = half_g + half_g * jnp.tanh(g * (c0 + (c0 * 0.044715) * (g * g)))
        y = h * gelu
        for j in range(slabs):
            y_ref[n * slabs + j, pl.ds(row0 + tau, R, stride=L), :] = y[:, j * LANES:(j + 1) * LANES]


def _rglru(x_lru, g_lru, conv_w, conv_b, wa_blk, ba, wx_blk, bx, lam, batch, seq):
    ts = LRU_CHUNKS * LRU_CHUNK_LEN
    step_rows = ts * LRU_BLOCKS_PER_STEP
    nblk = seq // step_rows
    n_slab = LRU_WIDTH // LANES
    n_grp = LRU_WIDTH // MXU_DIM
    per_group = lambda v: v.reshape(-1, n_grp, MXU_DIM).transpose(1, 0, 2)
    conv_w, conv_b, ba, bx, lam = (per_group(v) for v in (conv_w, conv_b, ba, bx, lam))
    tile = pl.BlockSpec((n_slab, step_rows, LANES), lambda b, t: (0, b * nblk + t, 0))
    return pl.pallas_call(
        _rglru_kernel,
        grid=(batch, nblk),
        in_specs=[
            tile, tile,
            _const_spec(conv_w.shape), _const_spec(conv_b.shape),
            _const_spec(wa_blk.shape), _const_spec(ba.shape),
            _const_spec(wx_blk.shape), _const_spec(bx.shape), _const_spec(lam.shape),
        ],
        out_specs=tile,
        out_shape=jax.ShapeDtypeStruct((n_slab, batch * seq, LANES), F32),
        scratch_shapes=[
            pltpu.VMEM((ts, MXU_DIM), F32),
            pltpu.VMEM((ts, MXU_DIM), F32),
            pltpu.VMEM((n_grp, CONV_WIDTH - 1, MXU_DIM), F32),
            pltpu.VMEM((n_grp, 1, MXU_DIM), F32),
        ],
        compiler_params=_params("parallel", "arbitrary"),
        name="rglru",
    )(x_lru, g_lru, conv_w, conv_b, wa_blk, ba, wx_blk, bx, lam)


def _merge_kernel(x_ref, ya_ref, yb_ref, ga_ref, gb_ref, wpa_ref, wpl_ref, wo_ref, o_ref):
    pa = _dot(ya_ref[...], wpa_ref[...])
    pb = _dot(_load_slabs(yb_ref).astype(BF16), wpl_ref[...])
    merged = _sigmoid(ga_ref[...].astype(F32)) * pa + _sigmoid(gb_ref[...].astype(F32)) * pb
    o_ref[...] = x_ref[...] + _dot(merged.astype(BF16), wo_ref[...])


def _merge(x2d, ya, yb, ga, gb, wpa, wpl, wo):
    m = x2d.shape[0]
    row = lambda i: (i, 0)
    tile = pl.BlockSpec((TM_PROJ, D_MODEL), row)
    slabs = pl.BlockSpec((yb.shape[0], TM_PROJ, LANES), lambda i: (0, i, 0))
    return pl.pallas_call(
        _merge_kernel,
        grid=(m // TM_PROJ,),
        in_specs=[tile, tile, slabs, tile, tile]
        + [_const_spec(wpa.shape), _const_spec(wpl.shape), _const_spec(wo.shape)],
        out_specs=tile,
        out_shape=jax.ShapeDtypeStruct((m, D_MODEL), F32),
        compiler_params=_params("parallel"),
        name="merge",
    )(x2d, ya, yb, ga, gb, wpa, wpl, wo)


def _ffn_kernel(x_ref, g_ref, wg_ref, wu_ref, wd_ref, o_ref):
    x = x_ref[...]
    h = _rms_rows(x, g_ref[...]).astype(BF16)
    gate = _dot(h, wg_ref[...])
    up = _dot(h, wu_ref[...])
    act = (gate * _sigmoid(gate) * up).astype(BF16)
    o_ref[...] = x + _dot(act, wd_ref[...])


def _ffn(x2d, norm_g, wg, wu, wd):
    m = x2d.shape[0]
    tm = TM_FFN
    row = lambda i: (i, 0)
    tile = pl.BlockSpec((tm, D_MODEL), row)
    return pl.pallas_call(
        _ffn_kernel,
        grid=(m // tm,),
        in_specs=[tile, _const_spec(norm_g.shape), _const_spec(wg.shape), _const_spec(wu.shape),
                  _const_spec(wd.shape)],
        out_specs=tile,
        out_shape=jax.ShapeDtypeStruct((m, D_MODEL), F32),
        compiler_params=_params("parallel"),
        name="ffn",
    )(x2d, norm_g, wg, wu, wd)


def _block_diag_groups(w):
    per = MXU_DIM // LRU_BLOCK
    n_grp = w.shape[0] // per
    w4 = w.reshape(n_grp, per, LRU_BLOCK, LRU_BLOCK)
    eye = jnp.eye(per, dtype=w.dtype)
    out = jnp.einsum("gpde,pq->gpdqe", w4, eye)
    return out.reshape(n_grp, MXU_DIM, MXU_DIM)


def kernel(x, positions, norm_mix_g, w_in, q_lat_g, w_q_up, kv_lat_g, w_kv_up, q_head_g, k_head_g,
           conv_w, conv_b, lru_wa, lru_ba, lru_wx, lru_bx, lru_lambda, w_proj_attn, w_proj_lru,
           w_out, norm_ffn_g, w_ffn_gate, w_ffn_up, w_ffn_down):
    batch, seq, d = x.shape
    depth = w_in.shape[0]
    half = HALF_ROPE
    inv_freq = (ROPE_THETA ** (-jnp.arange(half, dtype=F32) / half)).reshape(half, 1)
    positions3 = positions.reshape(batch, 1, seq)
    x2d = x.reshape(batch * seq, d)

    for l in range(depth):
        w = w_in[l].astype(BF16)
        c0 = Q_RANK
        c1 = c0 + KV_RANK
        c2 = c1 + ROPE
        c3 = c2 + LRU_WIDTH
        c4 = c3 + LRU_WIDTH
        c5 = c4 + D_MODEL
        w_kr = jnp.pad(w[:, c1:c2], ((0, 0), (0, LANES - ROPE)))
        w_parts = [w[:, :c0], w[:, c0:c1], w_kr, w[:, c2:c3], w[:, c3:c4], w[:, c4:c5], w[:, c5:]]
        cq, ckv, kr, x_lru, g_lru, gate_a, gate_b = _in_proj(x2d, norm_mix_g[l].reshape(1, d), w_parts)

        wkv = w_kv_up[l].astype(BF16).reshape(KV_RANK, HEADS, NOPE + V_DIM)
        wkn = wkv[:, :, :NOPE].reshape(KV_RANK, HEADS * NOPE)
        wv_t = wkv[:, :, NOPE:].reshape(KV_RANK, HEADS * V_DIM).T
        wq_t = w_q_up[l].astype(BF16).T
        amax = lambda v: jnp.max(jnp.abs(v))
        gq, gk = q_head_g[l], k_head_g[l]
        bound2 = Q_SCALE * (NOPE * amax(gq[:NOPE]) * amax(gk[:NOPE]) + ROPE * amax(gq[NOPE:]) * amax(gk[NOPE:]))
        shift_ok = bound2 <= SHIFT_LIMIT_LOG2
        offset = jnp.where(shift_ok, -bound2, 0.0).reshape(1, 1)
        q_t, kn, krope, v_t = _mla_prep(
            cq, ckv, kr, positions3, inv_freq, offset,
            q_lat_g[l].reshape(1, Q_RANK), kv_lat_g[l].reshape(1, KV_RANK), wq_t, wkn, wv_t,
            (gq[:NOPE] * Q_SCALE).reshape(NOPE, 1), (gq[NOPE:] * Q_SCALE).reshape(ROPE, 1),
            k_head_g[l][:NOPE].reshape(1, NOPE), k_head_g[l][NOPE:].reshape(ROPE, 1),
            batch, seq)
        y_a = lax.cond(
            shift_ok,
            functools.partial(_attention_shifted, batch=batch, seq=seq),
            functools.partial(_attention_online, batch=batch, seq=seq),
            q_t, kn, krope, v_t).reshape(batch * seq, HEADS * V_DIM)

        y_b = _rglru(
            x_lru, g_lru, conv_w[l], conv_b[l].reshape(1, LRU_WIDTH),
            _block_diag_groups(lru_wa[l] * -LOG2_E).astype(BF16), (lru_ba[l] * -LOG2_E).reshape(1, LRU_WIDTH),
            _block_diag_groups(lru_wx[l] * -LOG2_E).astype(BF16), (lru_bx[l] * -LOG2_E).reshape(1, LRU_WIDTH),
            lru_lambda[l].reshape(1, LRU_WIDTH), batch, seq)

        x2d = _merge(x2d, y_a, y_b, gate_a, gate_b, w_proj_attn[l].astype(BF16),
                     w_proj_lru[l].astype(BF16), w_out[l].astype(BF16))
        x2d = _ffn(x2d, norm_ffn_g[l].reshape(1, d), w_ffn_gate[l].astype(BF16),
                   w_ffn_up[l].astype(BF16), w_ffn_down[l].astype(BF16))
    return x2d.reshape(batch, seq, d)
```

```python
import functools
import math

import jax
import jax.numpy as jnp
from jax import lax
from jax.experimental import pallas as pl
from jax.experimental.pallas import tpu as pltpu

D_MODEL = 1024
HEADS = 8
NOPE = 128
ROPE = 64
HALF_ROPE = ROPE // 2
QK_DIM = NOPE + ROPE
V_DIM = 128
Q_RANK = 256
KV_RANK = 256
ROPE_THETA = 10000.0
LRU_WIDTH = 1024
LRU_BLOCK = 64
CONV_WIDTH = 4
LRU_C = 8.0
EPS = 1e-6

LANES = 128
SUBLANES = 8
MXU_DIM = 256
QK_PAD = 2 * LANES
SUM_ROWS = 16

LOG2_E = math.log2(math.e)
Q_SCALE = QK_DIM ** -0.5 * LOG2_E
SHIFT_LIMIT_LOG2 = 60.0

VMEM_LIMIT = 56 * 1024 * 1024

TM_PROJ = 1024
TM_FFN = 512
TM_PREP = 2048
TQ = 512
TK = 512
LRU_CHUNKS = 16
LRU_CHUNK_LEN = 16
LRU_BLOCKS_PER_STEP = 4
SQRT_FLOOR = 1e-30
PAIRS_PER_TRIP = 34

F32 = jnp.float32
BF16 = jnp.bfloat16


def _params(*semantics):
    return pltpu.CompilerParams(dimension_semantics=semantics, vmem_limit_bytes=VMEM_LIMIT)


def _const_spec(shape):
    zeros = (0,) * len(shape)
    return pl.BlockSpec(shape, lambda *_: zeros, pipeline_mode=pl.Buffered(1))


def _sigmoid(v):
    return 1.0 / (1.0 + jnp.exp(-v))


def _rms_rows(v, gain_row):
    ms = jnp.mean(v * v, axis=-1, keepdims=True)
    return v * lax.rsqrt(ms + EPS) * gain_row


def _rms_cols(v, gain_col):
    ms = jnp.mean(v * v, axis=0, keepdims=True)
    return v * lax.rsqrt(ms + EPS) * gain_col


def _dot(a, b):
    return jnp.dot(a, b, preferred_element_type=F32)


def _dot_nt(a, b):
    return lax.dot_general(a, b, (((1,), (1,)), ((), ())), preferred_element_type=F32)


def _in_proj_kernel(x_ref, g_ref, wq_ref, wkv_ref, wkr_ref, wx_ref, wg_ref, wa_ref, wb_ref,
                    cq_ref, ckv_ref, kr_ref, xl_ref, gl_ref, ga_ref, gb_ref):
    h = _rms_rows(x_ref[...], g_ref[...]).astype(BF16)
    cq_ref[...] = _dot(h, wq_ref[...])
    ckv_ref[...] = _dot(h, wkv_ref[...])
    kr_ref[...] = _dot(h, wkr_ref[...])
    _store_slabs(xl_ref, _dot(h, wx_ref[...]))
    _store_slabs(gl_ref, _dot(h, wg_ref[...]))
    ga_ref[...] = _dot(h, wa_ref[...]).astype(BF16)
    gb_ref[...] = _dot(h, wb_ref[...]).astype(BF16)


def _store_slabs(ref, v):
    for j in range(ref.shape[0]):
        ref[j] = v[:, j * LANES:(j + 1) * LANES].astype(ref.dtype)


def _load_slabs(ref):
    return jnp.concatenate([ref[j] for j in range(ref.shape[0])], axis=1)


def _in_proj(x2d, norm_g, w_parts):
    m = x2d.shape[0]
    row = lambda i: (i, 0)
    slab = lambda i: (0, i, 0)
    n_slab = LRU_WIDTH // LANES
    plain = lambda n, dt: (pl.BlockSpec((TM_PROJ, n), row), jax.ShapeDtypeStruct((m, n), dt))
    slabs = (pl.BlockSpec((n_slab, TM_PROJ, LANES), slab), jax.ShapeDtypeStruct((n_slab, m, LANES), F32))
    outs = [plain(Q_RANK, F32), plain(KV_RANK, F32), plain(LANES, F32), slabs, slabs,
            plain(D_MODEL, BF16), plain(D_MODEL, BF16)]
    return pl.pallas_call(
        _in_proj_kernel,
        grid=(m // TM_PROJ,),
        in_specs=[pl.BlockSpec((TM_PROJ, D_MODEL), row), _const_spec((1, D_MODEL))]
        + [_const_spec(w.shape) for w in w_parts],
        out_specs=[o[0] for o in outs],
        out_shape=[o[1] for o in outs],
        compiler_params=_params("parallel"),
        name="in_proj",
    )(x2d, norm_g, *w_parts)


def _mla_prep_kernel(cq_ref, ckv_ref, kr_ref, pos_ref, freq_ref, off_ref, qlg_ref, kvlg_ref,
                     wqt_ref, wkn_ref, wvt_ref, gqn_ref, gqr_ref, gkn_ref, gkr_ref,
                     qt_ref, kn_ref, krope_ref, vt_ref):
    tm = cq_ref.shape[0]
    pad_row = lax.broadcasted_iota(jnp.int32, (QK_PAD - QK_DIM, tm), 0)
    ang = freq_ref[...] * pos_ref[0].astype(F32)
    cos = jnp.cos(ang)
    sin = jnp.sin(ang)

    def rope_cols(v):
        x1, x2 = v[:HALF_ROPE], v[HALF_ROPE:]
        return x1 * cos - x2 * sin, x2 * cos + x1 * sin

    cqn = _rms_rows(cq_ref[...], qlg_ref[...]).astype(BF16)
    q_t = _dot_nt(wqt_ref[...], cqn)
    pad = jnp.where(pad_row == 0, off_ref[...], 0.0).astype(BF16)
    for h in range(HEADS):
        base = h * QK_DIM
        qn = _rms_cols(q_t[base:base + NOPE], gqn_ref[...])
        qr = _rms_cols(q_t[base + NOPE:base + QK_DIM], gqr_ref[...])
        o1, o2 = rope_cols(qr)
        q_pad = jnp.concatenate([qn.astype(BF16), o1.astype(BF16), o2.astype(BF16), pad], axis=0)
        for c in range(qt_ref.shape[2]):
            qt_ref[0, h, c] = q_pad[:, c * TQ:(c + 1) * TQ]

    ckvn = _rms_rows(ckv_ref[...], kvlg_ref[...]).astype(BF16)
    kn_all = _dot(ckvn, wkn_ref[...])
    v_t = _dot_nt(wvt_ref[...], ckvn)
    n_chunks = vt_ref.shape[2]
    for h in range(HEADS):
        kn = _rms_rows(kn_all[:, h * NOPE:(h + 1) * NOPE], gkn_ref[...])
        kn_ref[0, h] = kn.astype(BF16)
        for c in range(n_chunks):
            vt_ref[0, h, c] = v_t[h * V_DIM:(h + 1) * V_DIM, c * TK:(c + 1) * TK].astype(BF16)

    kr_t = kr_ref[...].T
    krn = _rms_cols(kr_t[:ROPE], gkr_ref[...])
    o1, o2 = rope_cols(krn)
    kr_out = jnp.concatenate([o1, o2, jnp.where(pad_row == 0, 1.0, 0.0)], axis=0)
    krope_ref[0] = kr_out.T.astype(BF16)


def _mla_prep(cq, ckv, kr, positions3, inv_freq, offset, q_lat_g, kv_lat_g, wq_t, wkn, wv_t,
              gqn, gqr, gkn, gkr, batch, seq):
    tm = TM_PREP
    nblk = seq // tm
    tok = lambda b, i: (b * nblk + i, 0)
    in_specs = [
        pl.BlockSpec((tm, Q_RANK), tok),
        pl.BlockSpec((tm, KV_RANK), tok),
        pl.BlockSpec((tm, LANES), tok),
        pl.BlockSpec((1, 1, tm), lambda b, i: (b, 0, i)),
        _const_spec(inv_freq.shape), _const_spec(offset.shape),
        _const_spec(q_lat_g.shape), _const_spec(kv_lat_g.shape),
        _const_spec(wq_t.shape), _const_spec(wkn.shape), _const_spec(wv_t.shape),
        _const_spec(gqn.shape), _const_spec(gqr.shape), _const_spec(gkn.shape), _const_spec(gkr.shape),
    ]
    out_specs = [
        pl.BlockSpec((1, HEADS, tm // TQ, QK_PAD, TQ), lambda b, i: (b, 0, i, 0, 0)),
        pl.BlockSpec((1, HEADS, tm, NOPE), lambda b, i: (b, 0, i, 0)),
        pl.BlockSpec((1, tm, LANES), lambda b, i: (b, i, 0)),
        pl.BlockSpec((1, HEADS, tm // TK, V_DIM, TK), lambda b, i: (b, 0, i, 0, 0)),
    ]
    out_shape = [
        jax.ShapeDtypeStruct((batch, HEADS, seq // TQ, QK_PAD, TQ), BF16),
        jax.ShapeDtypeStruct((batch, HEADS, seq, NOPE), BF16),
        jax.ShapeDtypeStruct((batch, seq, LANES), BF16),
        jax.ShapeDtypeStruct((batch, HEADS, seq // TK, V_DIM, TK), BF16),
    ]
    return pl.pallas_call(
        _mla_prep_kernel,
        grid=(batch, nblk),
        in_specs=in_specs,
        out_specs=out_specs,
        out_shape=out_shape,
        compiler_params=_params("parallel", "parallel"),
        name="mla_prep",
    )(cq, ckv, kr, positions3, inv_freq, offset, q_lat_g, kv_lat_g, wq_t, wkn, wv_t, gqn, gqr, gkn, gkr)


def _scores_t(qt_ref, kn_ref, kr_ref, i, j):
    start = pl.multiple_of(j * TK, TK)
    k = jnp.concatenate([kn_ref[0, 0, pl.ds(start, TK), :], kr_ref[0, pl.ds(start, TK), :]], axis=1)
    return _dot(k, qt_ref[0, 0, i])


def _causal_mask(s, i, j):
    key = j * TK + lax.broadcasted_iota(jnp.int32, s.shape, 0)
    qry = i * TQ + lax.broadcasted_iota(jnp.int32, s.shape, 1)
    return jnp.where(key <= qry, s, -jnp.inf)


def _attention_online_kernel(qt_ref, kn_ref, kr_ref, vt_ref, o_ref, m_ref, l_ref, acc_ref):
    i = pl.program_id(2)
    m_ref[...] = jnp.full(m_ref.shape, -jnp.inf, F32)
    l_ref[...] = jnp.zeros(l_ref.shape, F32)
    acc_ref[...] = jnp.zeros(acc_ref.shape, F32)

    def step(j, masked):
        s = _scores_t(qt_ref, kn_ref, kr_ref, 0, j)
        if masked:
            s = _causal_mask(s, i, j)
        m_old = m_ref[...]
        m_new = jnp.maximum(m_old, jnp.max(s, axis=0, keepdims=True))
        alpha = jnp.exp2(m_old - m_new)
        p = jnp.exp2(s - m_new)
        l_ref[...] = alpha * l_ref[...] + jnp.sum(p, axis=0, keepdims=True)
        acc_ref[...] = alpha * acc_ref[...] + _dot(vt_ref[0, 0, j], p.astype(BF16))
        m_ref[...] = m_new

    def body(j, carry):
        step(j, masked=False)
        return carry

    lax.fori_loop(0, i, body, 0)
    step(i, masked=True)
    out_t = acc_ref[...] / l_ref[...]
    o_ref[0] = out_t.T.astype(o_ref.dtype)


def _attention_shifted_kernel(itab_ref, jtab_ref, qt_ref, kn_ref, kr_ref, vt_ref, o_ref,
                              s0_ref, s1_ref, p0_ref, p1_ref, acc_ref, *, n_pairs, diag_per_trip):
    ones = jnp.ones((SUM_ROWS, TK), BF16)
    s_slots = (s0_ref, s1_ref)
    p_slots = (p0_ref, p1_ref)
    half = TK // 2
    assert TQ == TK and half % LANES == 0 and diag_per_trip >= 2

    def scores(t, s_ref, diagonal):
        if not diagonal:
            s_ref[...] = _scores_t(qt_ref, kn_ref, kr_ref, itab_ref[t], jtab_ref[t])
            return
        start = pl.multiple_of(jtab_ref[t] * TK, TK)
        k = jnp.concatenate([kn_ref[0, 0, pl.ds(start, TK), :], kr_ref[0, pl.ds(start, TK), :]], axis=1)
        q_t = qt_ref[0, 0, itab_ref[t]]
        s_ref[0:half, :] = _dot(k[0:half], q_t)
        s_ref[half:, half:] = _dot(k[half:], q_t[:, half:])

    def probs(t, s_ref, p_ref, diagonal):
        if not diagonal:
            p_ref[...] = jnp.exp2(s_ref[...]).astype(BF16)
            return

        def masked_exp2(s):
            key = lax.broadcasted_iota(jnp.int32, s.shape, 0)
            qry = lax.broadcasted_iota(jnp.int32, s.shape, 1)
            return jnp.exp2(jnp.where(key <= qry, s, -jnp.inf)).astype(BF16)

        p_ref[0:half, :] = masked_exp2(s_ref[0:half, :])
        p_ref[half:, half:] = masked_exp2(s_ref[half:, half:])

    def values(t, p_ref, diagonal):
        i = itab_ref[t]
        v_aug = jnp.concatenate([vt_ref[0, 0, jtab_ref[t]], ones], axis=0)
        if not diagonal:
            acc_ref[i] += _dot(v_aug, p_ref[...])
            return
        acc_ref[i, :, 0:half] += _dot(v_aug[:, 0:half], p_ref[0:half, 0:half])
        acc_ref[i, :, half:] += _dot(v_aug, p_ref[:, half:])

    def is_diagonal(position):
        return position % PAIRS_PER_TRIP < diag_per_trip

    acc_ref[...] = jnp.zeros(acc_ref.shape, F32)
    scores(0, s0_ref, diagonal=True)
    scores(1, s1_ref, diagonal=True)
    probs(0, s0_ref, p0_ref, diagonal=True)

    def body(u, carry):
        for sub in range(PAIRS_PER_TRIP):
            t = PAIRS_PER_TRIP * u + sub
            slot = sub % 2
            scores(t + 2, s_slots[slot], is_diagonal(sub + 2))
            probs(t + 1, s_slots[1 - slot], p_slots[1 - slot], is_diagonal(sub + 1))
            values(t, p_slots[slot], is_diagonal(sub))
        return carry

    lax.fori_loop(0, n_pairs // PAIRS_PER_TRIP, body, 0)

    for i in range(acc_ref.shape[0]):
        out_t = acc_ref[i, 0:V_DIM, :] / acc_ref[i, V_DIM:V_DIM + 1, :]
        o_ref[0, i * TQ:(i + 1) * TQ, :] = out_t.T.astype(o_ref.dtype)


def _attention_shifted(q_t, kn, krope, v_t, batch, seq):
    nq = seq // TQ
    diag = [(i, i) for i in range(nq)]
    below = [(i, j) for i in range(nq) for j in range(i)]
    n_pairs = len(diag) + len(below)
    trips = n_pairs // PAIRS_PER_TRIP
    assert TQ == TK and n_pairs % PAIRS_PER_TRIP == 0 and PAIRS_PER_TRIP % 2 == 0 and nq % trips == 0
    diag_per_trip = nq // trips
    below_per_trip = len(below) // trips
    pairs = []
    for u in range(trips):
        pairs += diag[u * diag_per_trip:(u + 1) * diag_per_trip]
        pairs += below[u * below_per_trip:(u + 1) * below_per_trip]
    pairs = pairs + [pairs[-1]] * 2
    itab = jnp.asarray([p[0] for p in pairs], jnp.int32)
    jtab = jnp.asarray([p[1] for p in pairs], jnp.int32)
    grid_spec = pltpu.PrefetchScalarGridSpec(
        num_scalar_prefetch=2,
        grid=(batch, HEADS),
        in_specs=[
            pl.BlockSpec((1, 1, nq, QK_PAD, TQ), lambda b, h, *_: (b, h, 0, 0, 0)),
            pl.BlockSpec((1, 1, seq, NOPE), lambda b, h, *_: (b, h, 0, 0)),
            pl.BlockSpec((1, seq, LANES), lambda b, h, *_: (b, 0, 0)),
            pl.BlockSpec((1, 1, seq // TK, V_DIM, TK), lambda b, h, *_: (b, h, 0, 0, 0)),
        ],
        out_specs=pl.BlockSpec((1, seq, V_DIM), lambda b, h, *_: (b, 0, h)),
        scratch_shapes=[
            pltpu.VMEM((TK, TQ), F32), pltpu.VMEM((TK, TQ), F32),
            pltpu.VMEM((TK, TQ), BF16), pltpu.VMEM((TK, TQ), BF16),
            pltpu.VMEM((nq, V_DIM + SUM_ROWS, TQ), F32),
        ],
    )
    return pl.pallas_call(
        functools.partial(_attention_shifted_kernel, n_pairs=n_pairs, diag_per_trip=diag_per_trip),
        grid_spec=grid_spec,
        out_shape=jax.ShapeDtypeStruct((batch, seq, HEADS * V_DIM), BF16),
        compiler_params=_params("parallel", "parallel"),
        name="attention_shifted",
    )(itab, jtab, q_t, kn, krope, v_t)


def _attention_online(q_t, kn, krope, v_t, batch, seq):
    assert TQ == TK
    nq = seq // TQ
    return pl.pallas_call(
        _attention_online_kernel,
        grid=(batch, HEADS, nq),
        in_specs=[
            pl.BlockSpec((1, 1, 1, QK_PAD, TQ), lambda b, h, i: (b, h, i, 0, 0)),
            pl.BlockSpec((1, 1, seq, NOPE), lambda b, h, i: (b, h, 0, 0)),
            pl.BlockSpec((1, seq, LANES), lambda b, h, i: (b, 0, 0)),
            pl.BlockSpec((1, 1, seq // TK, V_DIM, TK), lambda b, h, i: (b, h, 0, 0, 0)),
        ],
        out_specs=pl.BlockSpec((1, TQ, V_DIM), lambda b, h, i: (b, i, h)),
        out_shape=jax.ShapeDtypeStruct((batch, seq, HEADS * V_DIM), BF16),
        scratch_shapes=[pltpu.VMEM((1, TQ), F32), pltpu.VMEM((1, TQ), F32), pltpu.VMEM((V_DIM, TQ), F32)],
        compiler_params=_params("parallel", "parallel", "arbitrary"),
        name="attention_online",
    )(q_t, kn, krope, v_t)


def _rglru_kernel(x_ref, g_ref, cw_ref, cb_ref, wa_ref, ba_ref, wx_ref, bx_ref, lam_ref,
                  y_ref, a_ref, b_ref, tail_ref, h_ref):
    t = pl.program_id(1)

    @pl.when(t == 0)
    def _():
        tail_ref[...] = jnp.zeros(tail_ref.shape, F32)
        h_ref[...] = jnp.zeros(h_ref.shape, F32)

    block_rows = LRU_CHUNKS * LRU_CHUNK_LEN
    for row0 in range(0, x_ref.shape[1], block_rows):
        for n in range(LRU_WIDTH // MXU_DIM):
            _rglru_group(n, row0, x_ref, g_ref, cw_ref, cb_ref, wa_ref, ba_ref, wx_ref, bx_ref, lam_ref,
                         y_ref, a_ref, b_ref, tail_ref, h_ref)


def _rglru_group(n, row0, x_ref, g_ref, cw_ref, cb_ref, wa_ref, ba_ref, wx_ref, bx_ref, lam_ref,
                 y_ref, a_ref, b_ref, tail_ref, h_ref):
    R, L, W = LRU_CHUNKS, LRU_CHUNK_LEN, MXU_DIM
    slabs = W // LANES
    chunk = lax.broadcasted_iota(jnp.int32, (R, W), 0)

    def chunk_rows(ref, tau):
        return jnp.concatenate(
            [ref[n * slabs + j, pl.ds(row0 + tau, R, stride=L), :] for j in range(slabs)], axis=1)

    def step_rows(v, tau):
        return v[tau * R:(tau + 1) * R]

    def from_prev_chunk(v, first):
        return jnp.where(chunk == 0, first, pltpu.roll(v, 1, 0))

    x = jnp.concatenate([chunk_rows(x_ref, tau) for tau in range(L)], axis=0)
    tails = tail_ref[n]
    lead = [from_prev_chunk(step_rows(x, L - m), tails[m - 1:m, :]) for m in range(CONV_WIDTH - 1, 0, -1)]
    ext = jnp.concatenate(lead + [x], axis=0)
    tail_ref[n] = jnp.concatenate(
        [x[(L - m + 1) * R - 1:(L - m + 1) * R] for m in range(1, CONV_WIDTH)], axis=0)

    cw = cw_ref[n]
    xc = cb_ref[n] + ext[0:L * R] * cw[0:1, :]
    for tap in range(1, CONV_WIDTH):
        xc = xc + ext[tap * R:(tap + L) * R] * cw[tap:tap + 1, :]

    xcb = xc.astype(BF16)
    r = 1.0 / (1.0 + jnp.exp2(_dot(xcb, wa_ref[n]) + ba_ref[n]))
    gi = 1.0 / (1.0 + jnp.exp2(_dot(xcb, wx_ref[n]) + bx_ref[n]))

    lam = lam_ref[n]
    softplus_neg = jnp.maximum(-lam, 0.0) + jnp.log1p(jnp.exp(-jnp.abs(lam)))
    neg_log_a = r * (LRU_C * softplus_neg)
    a = jnp.exp2(neg_log_a * (-LOG2_E))
    m2 = jnp.tanh(neg_log_a) * (1.0 + a * a)
    mult = m2 * lax.rsqrt(jnp.maximum(m2, SQRT_FLOOR))
    a_ref[...] = a
    b_ref[...] = mult * (gi * xc)

    a_cum = a_ref[0:R, :]
    h_loc = b_ref[0:R, :]
    for tau in range(1, L):
        a_t = a_ref[tau * R:(tau + 1) * R, :]
        h_loc = a_t * h_loc + b_ref[tau * R:(tau + 1) * R, :]
        a_cum = a_t * a_cum
        a_ref[tau * R:(tau + 1) * R, :] = a_cum
        b_ref[tau * R:(tau + 1) * R, :] = h_loc

    carry = h_ref[n]
    sh = 1
    while sh < R:
        keep = chunk >= sh
        a_prev = jnp.where(keep, pltpu.roll(a_cum, sh, 0), 1.0)
        h_prev = jnp.where(keep, pltpu.roll(h_loc, sh, 0), 0.0)
        h_loc = a_cum * h_prev + h_loc
        a_cum = a_cum * a_prev
        sh *= 2
    h_end = a_cum * carry + h_loc
    h_in = from_prev_chunk(h_end, carry)
    h_ref[n] = h_end[R - 1:R, :]

    c0 = math.sqrt(2.0 / math.pi)
    for tau in range(L):
        h = b_ref[tau * R:(tau + 1) * R, :] + a_ref[tau * R:(tau + 1) * R, :] * h_in
        g = chunk_rows(g_ref, tau)
        half_g = 0.5 * g
        gelu = half_g + half_g * jnp.tanh(g * (c0 + (c0 * 0.044715) * (g * g)))
        y = h * gelu
        for j in range(slabs):
            y_ref[n * slabs + j, pl.ds(row0 + tau, R, stride=L), :] = y[:, j * LANES:(j + 1) * LANES]


def _rglru(x_lru, g_lru, conv_w, conv_b, wa_blk, ba, wx_blk, bx, lam, batch, seq):
    ts = LRU_CHUNKS * LRU_CHUNK_LEN
    step_rows = ts * LRU_BLOCKS_PER_STEP
    nblk = seq // step_rows
    n_slab = LRU_WIDTH // LANES
    n_grp = LRU_WIDTH // MXU_DIM
    per_group = lambda v: v.reshape(-1, n_grp, MXU_DIM).transpose(1, 0, 2)
    conv_w, conv_b, ba, bx, lam = (per_group(v) for v in (conv_w, conv_b, ba, bx, lam))
    tile = pl.BlockSpec((n_slab, step_rows, LANES), lambda b, t: (0, b * nblk + t, 0))
    return pl.pallas_call(
        _rglru_kernel,
        grid=(batch, nblk),
        in_specs=[
            tile, tile,
            _const_spec(conv_w.shape), _const_spec(conv_b.shape),
            _const_spec(wa_blk.shape), _const_spec(ba.shape),
            _const_spec(wx_blk.shape), _const_spec(bx.shape), _const_spec(lam.shape),
        ],
        out_specs=tile,
        out_shape=jax.ShapeDtypeStruct((n_slab, batch * seq, LANES), F32),
        scratch_shapes=[
            pltpu.VMEM((ts, MXU_DIM), F32),
            pltpu.VMEM((ts, MXU_DIM), F32),
            pltpu.VMEM((n_grp, CONV_WIDTH - 1, MXU_DIM), F32),
            pltpu.VMEM((n_grp, 1, MXU_DIM), F32),
        ],
        compiler_params=_params("parallel", "arbitrary"),
        name="rglru",
    )(x_lru, g_lru, conv_w, conv_b, wa_blk, ba, wx_blk, bx, lam)


def _merge_kernel(x_ref, ya_ref, yb_ref, ga_ref, gb_ref, wpa_ref, wpl_ref, wo_ref, o_ref):
    pa = _dot(ya_ref[...], wpa_ref[...])
    pb = _dot(_load_slabs(yb_ref).astype(BF16), wpl_ref[...])
    merged = _sigmoid(ga_ref[...].astype(F32)) * pa + _sigmoid(gb_ref[...].astype(F32)) * pb
    o_ref[...] = x_ref[...] + _dot(merged.astype(BF16), wo_ref[...])


def _merge(x2d, ya, yb, ga, gb, wpa, wpl, wo):
    m = x2d.shape[0]
    row = lambda i: (i, 0)
    tile = pl.BlockSpec((TM_PROJ, D_MODEL), row)
    slabs = pl.BlockSpec((yb.shape[0], TM_PROJ, LANES), lambda i: (0, i, 0))
    return pl.pallas_call(
        _merge_kernel,
        grid=(m // TM_PROJ,),
        in_specs=[tile, tile, slabs, tile, tile]
        + [_const_spec(wpa.shape), _const_spec(wpl.shape), _const_spec(wo.shape)],
        out_specs=tile,
        out_shape=jax.ShapeDtypeStruct((m, D_MODEL), F32),
        compiler_params=_params("parallel"),
        name="merge",
    )(x2d, ya, yb, ga, gb, wpa, wpl, wo)


def _ffn_kernel(x_ref, g_ref, wg_ref, wu_ref, wd_ref, o_ref):
    x = x_ref[...]
    h = _rms_rows(x, g_ref[...]).astype(BF16)
    gate = _dot(h, wg_ref[...])
    up = _dot(h, wu_ref[...])
    act = (gate * _sigmoid(gate) * up).astype(BF16)
    o_ref[...] = x + _dot(act, wd_ref[...])


def _ffn(x2d, norm_g, wg, wu, wd):
    m = x2d.shape[0]
    tm = TM_FFN
    row = lambda i: (i, 0)
    tile = pl.BlockSpec((tm, D_MODEL), row)
    return pl.pallas_call(
        _ffn_kernel,
        grid=(m // tm,),
        in_specs=[tile, _const_spec(norm_g.shape), _const_spec(wg.shape), _const_spec(wu.shape),
                  _const_spec(wd.shape)],
        out_specs=tile,
        out_shape=jax.ShapeDtypeStruct((m, D_MODEL), F32),
        compiler_params=_params("parallel"),
        name="ffn",
    )(x2d, norm_g, wg, wu, wd)


def _block_diag_groups(w):
    per = MXU_DIM // LRU_BLOCK
    n_grp = w.shape[0] // per
    w4 = w.reshape(n_grp, per, LRU_BLOCK, LRU_BLOCK)
    eye = jnp.eye(per, dtype=w.dtype)
    out = jnp.einsum("gpde,pq->gpdqe", w4, eye)
    return out.reshape(n_grp, MXU_DIM, MXU_DIM)


def kernel(x, positions, norm_mix_g, w_in, q_lat_g, w_q_up, kv_lat_g, w_kv_up, q_head_g, k_head_g,
           conv_w, conv_b, lru_wa, lru_ba, lru_wx, lru_bx, lru_lambda, w_proj_attn, w_proj_lru,
           w_out, norm_ffn_g, w_ffn_gate, w_ffn_up, w_ffn_down):
    batch, seq, d = x.shape
    depth = w_in.shape[0]
    half = HALF_ROPE
    inv_freq = (ROPE_THETA ** (-jnp.arange(half, dtype=F32) / half)).reshape(half, 1)
    positions3 = positions.reshape(batch, 1, seq)
    x2d = x.reshape(batch * seq, d)

    for l in range(depth):
        w = w_in[l].astype(BF16)
        c0 = Q_RANK
        c1 = c0 + KV_RANK
        c2 = c1 + ROPE
        c3 = c2 + LRU_WIDTH
        c4 = c3 + LRU_WIDTH
        c5 = c4 + D_MODEL
        w_kr = jnp.pad(w[:, c1:c2], ((0, 0), (0, LANES - ROPE)))
        w_parts = [w[:, :c0], w[:, c0:c1], w_kr, w[:, c2:c3], w[:, c3:c4], w[:, c4:c5], w[:, c5:]]
        cq, ckv, kr, x_lru, g_lru, gate_a, gate_b = _in_proj(x2d, norm_mix_g[l].reshape(1, d), w_parts)

        wkv = w_kv_up[l].astype(BF16).reshape(KV_RANK, HEADS, NOPE + V_DIM)
        wkn = wkv[:, :, :NOPE].reshape(KV_RANK, HEADS * NOPE)
        wv_t = wkv[:, :, NOPE:].reshape(KV_RANK, HEADS * V_DIM).T
        wq_t = w_q_up[l].astype(BF16).T
        amax = lambda v: jnp.max(jnp.abs(v))
        gq, gk = q_head_g[l], k_head_g[l]
        bound2 = Q_SCALE * (NOPE * amax(gq[:NOPE]) * amax(gk[:NOPE]) + ROPE * amax(gq[NOPE:]) * amax(gk[NOPE:]))
        shift_ok = bound2 <= SHIFT_LIMIT_LOG2
        offset = jnp.where(shift_ok, -bound2, 0.0).reshape(1, 1)
        q_t, kn, krope, v_t = _mla_prep(
            cq, ckv, kr, positions3, inv_freq, offset,
            q_lat_g[l].reshape(1, Q_RANK), kv_lat_g[l].reshape(1, KV_RANK), wq_t, wkn, wv_t,
            (gq[:NOPE] * Q_SCALE).reshape(NOPE, 1), (gq[NOPE:] * Q_SCALE).reshape(ROPE, 1),
            k_head_g[l][:NOPE].reshape(1, NOPE), k_head_g[l][NOPE:].reshape(ROPE, 1),
            batch, seq)
        y_a = lax.cond(
            shift_ok,
            functools.partial(_attention_shifted, batch=batch, seq=seq),
            functools.partial(_attention_online, batch=batch, seq=seq),
            q_t, kn, krope, v_t).reshape(batch * seq, HEADS * V_DIM)

        y_b = _rglru(
            x_lru, g_lru, conv_w[l], conv_b[l].reshape(1, LRU_WIDTH),
            _block_diag_groups(lru_wa[l] * -LOG2_E).astype(BF16), (lru_ba[l] * -LOG2_E).reshape(1, LRU_WIDTH),
            _block_diag_groups(lru_wx[l] * -LOG2_E).astype(BF16), (lru_bx[l] * -LOG2_E).reshape(1, LRU_WIDTH),
            lru_lambda[l].reshape(1, LRU_WIDTH), batch, seq)

        x2d = _merge(x2d, y_a, y_b, gate_a, gate_b, w_proj_attn[l].astype(BF16),
                     w_proj_lru[l].astype(BF16), w_out[l].astype(BF16))
        x2d = _ffn(x2d, norm_ffn_g[l].reshape(1, d), w_ffn_gate[l].astype(BF16),
                   w_ffn_up[l].astype(BF16), w_ffn_down[l].astype(BF16))
    return x2d.reshape(batch, seq, d)
```
